```python
import jax, jax.numpy as jnp
from jax import lax
import numpy as np

D_MODEL = 1024
BATCH = 2
SEQ = 8192
DEPTH = 2

ATTN_GROUPS = ((128, 1), (512, 4), (2048, 16))
N_GROUPS = 3
ATTN_HEADS = 8
ATTN_HEAD_DIM = 128
ATTN_WIDTH = ATTN_HEADS * ATTN_HEAD_DIM
RET_HEADS = 4
RET_KEY_DIM = 256
RET_VAL_DIM = 512
RET_QK_WIDTH = RET_HEADS * RET_KEY_DIM
RET_V_WIDTH = RET_HEADS * RET_VAL_DIM
RET_CHUNK = 128
ROPE_BASE = 10000.0
D_FF = 2816
CONV_WIDTH = 3
PLE_DIM = 256
MAX_POS_OFFSET = 4096
IN_SIZES = (N_GROUPS * 3 * ATTN_WIDTH, RET_QK_WIDTH, RET_QK_WIDTH, RET_V_WIDTH, RET_V_WIDTH, D_MODEL, D_MODEL)
N_IN = sum(IN_SIZES)
DN_ALPHA = (2 * DEPTH) ** 0.25
DN_BETA = (8 * DEPTH) ** -0.25
LN_EPS = 1e-5
GN_EPS = 1e-6

kernel_name = 'hybrid_dilated_attn_retention_convffn_deepnorm'


def layer_norm(x, g, b):
    xf = x.astype(jnp.float32)
    mu = xf.mean(-1, keepdims=True)
    var = jnp.square(xf - mu).mean(-1, keepdims=True)
    return ((xf - mu) * lax.rsqrt(var + LN_EPS) * g + b).astype(x.dtype)


def dilated_window_attention(q, k, v, window, dilation):
    b, s, h, dh = q.shape
    span = window // dilation
    blk = span
    sub_len = -(-s // dilation)
    sub_len = -(-sub_len // blk) * blk
    s_pad = sub_len * dilation
    nb = sub_len // blk

    def to_blocks(t):
        t = jnp.pad(t.astype(jnp.float32), ((0, 0), (0, s_pad - s), (0, 0), (0, 0)))
        t = t.reshape(b, sub_len, dilation, h, dh).transpose(0, 2, 3, 1, 4)
        return t.reshape(b, dilation, h, nb, blk, dh)

    def with_prev(t):
        prev = jnp.pad(t, ((0, 0), (0, 0), (0, 0), (1, 0), (0, 0), (0, 0)))[:, :, :, :-1]
        return jnp.concatenate([prev, t], axis=4)

    qb = to_blocks(q)
    kc = with_prev(to_blocks(k))
    vc = with_prev(to_blocks(v))
    scores = jnp.einsum('brhnqd,brhnkd->brhnqk', qb, kc) * (dh ** -0.5)
    q_idx = jnp.arange(blk)[:, None] + blk
    k_idx = jnp.arange(2 * blk)[None, :]
    dist = q_idx - k_idx
    band = (dist >= 0) & (dist <= span)
    valid = (jnp.arange(nb)[:, None, None] * blk + k_idx[None] - blk) >= 0
    mask = band[None] & valid
    scores = jnp.where(mask, scores, -jnp.inf)
    m = scores.max(-1, keepdims=True)
    pr = jnp.exp(scores - m)
    denom = pr.sum(-1, keepdims=True)
    out = jnp.einsum('brhnqk,brhnkd->brhnqd', pr, vc) / denom
    lse = (m + jnp.log(denom))[..., 0]
    out = out.reshape(b, dilation, h, sub_len, dh).transpose(0, 3, 1, 2, 4).reshape(b, s_pad, h, dh)[:, :s]
    lse = lse.reshape(b, dilation, h, sub_len).transpose(0, 3, 1, 2).reshape(b, s_pad, h)[:, :s]
    return out, lse


def rotary(t, positions):
    half = t.shape[-1] // 2
    freq = jnp.power(ROPE_BASE, -jnp.arange(half, dtype=jnp.float32) / half)
    ang = positions.astype(jnp.float32)[:, :, None, None] * freq
    cos, sin = jnp.cos(ang), jnp.sin(ang)
    t1, t2 = t[..., :half], t[..., half:]
    return jnp.concatenate([t1 * cos - t2 * sin, t1 * sin + t2 * cos], axis=-1)


def chunkwise_retention(q, k, v, positions):
    b, s, h, dk = q.shape
    dv = v.shape[-1]
    c = RET_CHUNK
    nc = s // c
    q = rotary(q.astype(jnp.float32), positions)
    k = rotary(k.astype(jnp.float32), positions) * (dk ** -0.5)
    v = v.astype(jnp.float32)
    log_gamma = jnp.log1p(-jnp.exp2(-5.0 - jnp.arange(h, dtype=jnp.float32)))
    idx = jnp.arange(c, dtype=jnp.float32)
    rel = idx[:, None] - idx[None, :]
    intra_decay = jnp.where(rel >= 0, jnp.exp(log_gamma[:, None, None] * jnp.maximum(rel, 0.0)), 0.0)
    query_decay = jnp.exp(log_gamma[:, None] * (idx + 1.0))[:, :, None]
    key_decay = jnp.exp(log_gamma[:, None] * (c - 1.0 - idx))[:, :, None]
    chunk_decay = jnp.exp(log_gamma * c)[:, None, None]

    def chunks(t):
        return t.reshape(b, nc, c, h, t.shape[-1]).transpose(1, 0, 3, 2, 4)

    def step(state, qkv):
        qc, kc, vc = qkv
        inner = jnp.einsum('bhqk,bhkv->bhqv', jnp.einsum('bhqd,bhkd->bhqk', qc, kc) * intra_decay, vc)
        cross = jnp.einsum('bhqd,bhdv->bhqv', qc, state) * query_decay
        state = chunk_decay * state + jnp.einsum('bhkd,bhkv->bhdv', kc, vc * key_decay)
        return state, inner + cross

    state0 = jnp.zeros((b, h, dk, dv), jnp.float32)
    _, out = lax.scan(step, state0, (chunks(q), chunks(k), chunks(v)))
    return out.transpose(1, 0, 3, 2, 4).reshape(b, s, h, dv)


def head_group_norm(y, g, bias):
    b, s, h, dv = y.shape
    mu = y.mean(-1, keepdims=True)
    var = jnp.square(y - mu).mean(-1, keepdims=True)
    y = (y - mu) * lax.rsqrt(var + GN_EPS)
    return y.reshape(b, s, h * dv) * g + bias


def token_mixer(x, positions, w_in, w_attn_proj, w_ret_proj, ret_gn_g, ret_gn_b, w_out):
    b, s, _ = x.shape
    proj = x @ w_in
    offsets = [int(o) for o in np.cumsum(IN_SIZES)[:-1]]
    attn_qkv, ret_q, ret_k, ret_v, ret_g, gate_a, gate_r = jnp.split(proj, offsets, axis=-1)
    attn_qkv = attn_qkv.reshape(b, s, N_GROUPS, 3, ATTN_HEADS, ATTN_HEAD_DIM)
    outs, lses = [], []
    for gi, (window, dilation) in enumerate(ATTN_GROUPS):
        o, l = dilated_window_attention(attn_qkv[:, :, gi, 0], attn_qkv[:, :, gi, 1], attn_qkv[:, :, gi, 2], window, dilation)
        outs.append(o)
        lses.append(l)
    weights = jax.nn.softmax(jnp.stack(lses, 0), axis=0)[..., None]
    attn = (weights * jnp.stack(outs, 0)).sum(0).reshape(b, s, ATTN_WIDTH).astype(x.dtype)
    ret = chunkwise_retention(ret_q.reshape(b, s, RET_HEADS, RET_KEY_DIM), ret_k.reshape(b, s, RET_HEADS, RET_KEY_DIM), ret_v.reshape(b, s, RET_HEADS, RET_VAL_DIM), positions)
    ret = head_group_norm(ret, ret_gn_g, ret_gn_b)
    ret = (jax.nn.silu(ret_g.astype(jnp.float32)) * ret).astype(x.dtype)
    merged = jax.nn.sigmoid(gate_a) * (attn @ w_attn_proj) + jax.nn.sigmoid(gate_r) * (ret @ w_ret_proj)
    return merged @ w_out


def conv_ffn(x, w_up, conv_w, conv_b, w_down):
    h = x @ w_up
    h = lax.conv_general_dilated(h, conv_w[:, None, :].astype(h.dtype), window_strides=(1,), padding=((CONV_WIDTH - 1, 0),), dimension_numbers=('NWC', 'WIO', 'NWC'), feature_group_count=h.shape[-1]) + conv_b
    gate, up = jnp.split(h, 2, axis=-1)
    return (jax.nn.gelu(gate) * up) @ w_down


def setup_inputs(seed: int = 0) -> dict:
    key = jax.random.key(seed)
    ks = jax.random.split(key, 20)
    f32 = jnp.float32

    def nrm(k, shape, scale):
        return jax.random.normal(k, shape, f32) * scale

    x = nrm(ks[0], (BATCH, SEQ, D_MODEL), 1.0)
    p = nrm(ks[1], (DEPTH, BATCH, SEQ, PLE_DIM), 1.0)
    offset = jax.random.randint(ks[2], (BATCH, 1), 0, MAX_POS_OFFSET, dtype=jnp.int32)
    positions = offset + jnp.arange(SEQ, dtype=jnp.int32)[None, :]
    w_in = nrm(ks[3], (DEPTH, D_MODEL, N_IN), D_MODEL ** -0.5)
    w_attn_proj = nrm(ks[4], (DEPTH, ATTN_WIDTH, D_MODEL), ATTN_WIDTH ** -0.5)
    w_ret_proj = nrm(ks[5], (DEPTH, RET_V_WIDTH, D_MODEL), RET_V_WIDTH ** -0.5)
    ret_gn_g = 1.0 + nrm(ks[6], (DEPTH, RET_V_WIDTH), 0.02)
    ret_gn_b = nrm(ks[7], (DEPTH, RET_V_WIDTH), 0.02)
    w_out = nrm(ks[8], (DEPTH, D_MODEL, D_MODEL), D_MODEL ** -0.5 * DN_BETA)
    ln1_g = 1.0 + nrm(ks[9], (DEPTH, D_MODEL), 0.02)
    ln1_b = nrm(ks[10], (DEPTH, D_MODEL), 0.02)
    w_up = nrm(ks[11], (DEPTH, D_MODEL, 2 * D_FF), D_MODEL ** -0.5)
    conv_w = nrm(ks[12], (DEPTH, CONV_WIDTH, 2 * D_FF), CONV_WIDTH ** -0.5)
    conv_b = nrm(ks[13], (DEPTH, 2 * D_FF), 0.02)
    w_down = nrm(ks[14], (DEPTH, D_FF, D_MODEL), D_FF ** -0.5 * DN_BETA)
    w_ple_gate = nrm(ks[15], (DEPTH, D_MODEL, D_MODEL), D_MODEL ** -0.5)
    w_ple_proj = nrm(ks[16], (DEPTH, PLE_DIM, D_MODEL), PLE_DIM ** -0.5 * DN_BETA)
    ln2_g = 1.0 + nrm(ks[17], (DEPTH, D_MODEL), 0.02)
    ln2_b = nrm(ks[18], (DEPTH, D_MODEL), 0.02)
    return {'x': x, 'p': p, 'positions': positions, 'w_in': w_in, 'w_attn_proj': w_attn_proj, 'w_ret_proj': w_ret_proj, 'ret_gn_g': ret_gn_g, 'ret_gn_b': ret_gn_b, 'w_out': w_out, 'ln1_g': ln1_g, 'ln1_b': ln1_b, 'w_up': w_up, 'conv_w': conv_w, 'conv_b': conv_b, 'w_down': w_down, 'w_ple_gate': w_ple_gate, 'w_ple_proj': w_ple_proj, 'ln2_g': ln2_g, 'ln2_b': ln2_b}


def reference(x, p, positions, w_in, w_attn_proj, w_ret_proj, ret_gn_g, ret_gn_b, w_out, ln1_g, ln1_b, w_up, conv_w, conv_b, w_down, w_ple_gate, w_ple_proj, ln2_g, ln2_b):
    for i in range(DEPTH):
        mix = token_mixer(x, positions, w_in[i], w_attn_proj[i], w_ret_proj[i], ret_gn_g[i], ret_gn_b[i], w_out[i])
        x = layer_norm(DN_ALPHA * x + mix, ln1_g[i], ln1_b[i])
        ple = jax.nn.sigmoid(x @ w_ple_gate[i]) * (p[i] @ w_ple_proj[i])
        x = layer_norm(DN_ALPHA * x + conv_ffn(x, w_up[i], conv_w[i], conv_b[i], w_down[i]) + ple, ln2_g[i], ln2_b[i])
    return x
```

```python
import functools

import jax
import jax.numpy as jnp
from jax import lax
from jax.experimental import pallas as pl
from jax.experimental.pallas import tpu as pltpu

F32 = jnp.float32
BF16 = jnp.bfloat16

D_MODEL = 1024
ATTN_GROUPS = ((128, 1), (512, 4), (2048, 16))
ATTN_HEADS = 8
HEAD_DIM = 128
ATTN_WIDTH = ATTN_HEADS * HEAD_DIM
RET_HEADS = 4
RET_KEY_DIM = 256
RET_VAL_DIM = 512
RET_CHUNK = 128
ROPE_BASE = 10000.0
D_FF = 2816
PLE_DIM = 256
N_IN = 17408
DEPTH = 2
DN_ALPHA = (2 * DEPTH) ** 0.25
LN_EPS = 1e-5
GN_EPS = 1e-6

COL_RET_Q = 3 * 3 * ATTN_WIDTH
COL_RET_K = COL_RET_Q + RET_HEADS * RET_KEY_DIM
COL_RET_V = COL_RET_K + RET_HEADS * RET_KEY_DIM
COL_RET_G = COL_RET_V + RET_HEADS * RET_VAL_DIM
COL_GATE_A = COL_RET_G + RET_HEADS * RET_VAL_DIM
COL_GATE_R = COL_GATE_A + D_MODEL

LANES = 128
ATTN_BLK = 128
VMEM_LIMIT = 56 * 1024 * 1024


def _params(sem, vmem=VMEM_LIMIT):
    return pltpu.CompilerParams(dimension_semantics=sem, vmem_limit_bytes=vmem)


def _dot(a, b):
    return jnp.dot(a, b, preferred_element_type=F32)


def _dot_nt(a, b):
    return lax.dot_general(a, b, (((1,), (1,)), ((), ())), preferred_element_type=F32)


def _dot_tn(a, b):
    return lax.dot_general(a, b, (((0,), (0,)), ((), ())), preferred_element_type=F32)


def _sigmoid(x):
    return 1.0 / (1.0 + jnp.exp(-x))


def _layer_norm(y, g, b):
    mu = jnp.mean(y, axis=-1, keepdims=True)
    d = y - mu
    var = jnp.mean(d * d, axis=-1, keepdims=True)
    return d * lax.rsqrt(var + LN_EPS) * g + b


def _rope_kernel(pos_ref, freq_ref, cos_ref, sin_ref):
    ang = pos_ref[...] * freq_ref[...]
    cos_ref[...] = jnp.cos(ang)
    sin_ref[...] = jnp.sin(ang)


def _rope_tables(positions):
    b, s = positions.shape
    half = RET_KEY_DIM // 2
    pos = jnp.broadcast_to(positions.astype(F32).reshape(b * s, 1), (b * s, half))
    freq = jnp.power(ROPE_BASE, -jnp.arange(half, dtype=F32) / half).reshape(1, half)
    rows = 2048
    spec = pl.BlockSpec((rows, half), lambda i: (i, 0))
    cos, sin = pl.pallas_call(
        _rope_kernel,
        grid=(b * s // rows,),
        in_specs=[spec, pl.BlockSpec((1, half), lambda i: (0, 0))],
        out_specs=[spec, spec],
        out_shape=[jax.ShapeDtypeStruct((b * s, half), F32)] * 2,
        compiler_params=_params(("arbitrary",)),
        name="rope_tables",
    )(pos, freq)
    return cos.reshape(b, s, half), sin.reshape(b, s, half)


def _proj_kernel(x_ref, w_ref, o_ref, xb_ref):
    @pl.when(pl.program_id(1) == 0)
    def _():
        xb_ref[...] = x_ref[...].astype(BF16)

    o_ref[...] = _dot(xb_ref[...], w_ref[...]).astype(o_ref.dtype)


def _in_proj(x2d, w_bf16, tm=1024, tn=1024):
    t, d = x2d.shape
    n = w_bf16.shape[1]
    return pl.pallas_call(
        _proj_kernel,
        grid=(t // tm, n // tn),
        in_specs=[pl.BlockSpec((tm, d), lambda i, j: (i, 0)),
                  pl.BlockSpec((d, tn), lambda i, j: (0, j))],
        out_specs=pl.BlockSpec((tm, tn), lambda i, j: (i, j)),
        out_shape=jax.ShapeDtypeStruct((t, n), BF16),
        scratch_shapes=[pltpu.VMEM((tm, d), BF16)],
        compiler_params=_params(("arbitrary", "arbitrary")),
        name="in_proj",
    )(x2d, w_bf16)


def _attn_kernel(q_ref, k_ref, v_ref, kp_ref, vp_ref, o_ref, lse_ref, *, tq):
    n = pl.program_id(2)
    scale = HEAD_DIM ** -0.5
    row = lax.broadcasted_iota(jnp.int32, (ATTN_BLK, ATTN_BLK), 0)
    col = lax.broadcasted_iota(jnp.int32, (ATTN_BLK, ATTN_BLK), 1)
    cur_mask = col <= row
    prev_band = col >= row
    has_prev = n > 0
    for s in range(tq // ATTN_BLK):
        rows = slice(s * ATTN_BLK, (s + 1) * ATTN_BLK)
        lse_tile = jnp.zeros((ATTN_BLK, LANES), F32)
        for h in range(ATTN_HEADS):
            hs = slice(h * HEAD_DIM, (h + 1) * HEAD_DIM)
            q = q_ref[0, rows, hs]
            kc = k_ref[0, rows, hs]
            vc = v_ref[0, rows, hs]
            if s == 0:
                kp = kp_ref[0, :, hs]
                vp = vp_ref[0, :, hs]
                prev_mask = jnp.logical_and(prev_band, has_prev)
            else:
                prows = slice((s - 1) * ATTN_BLK, s * ATTN_BLK)
                kp = k_ref[0, prows, hs]
                vp = v_ref[0, prows, hs]
                prev_mask = prev_band
            sc = jnp.where(cur_mask, _dot_nt(q, kc) * scale, -jnp.inf)
            sp = jnp.where(prev_mask, _dot_nt(q, kp) * scale, -jnp.inf)
            m = jnp.maximum(jnp.max(sc, axis=-1, keepdims=True), jnp.max(sp, axis=-1, keepdims=True))
            pc = jnp.exp(sc - m)
            pp = jnp.exp(sp - m)
            l = jnp.sum(pc, axis=-1, keepdims=True) + jnp.sum(pp, axis=-1, keepdims=True)
            acc = _dot(pc.astype(BF16), vc) + _dot(pp.astype(BF16), vp)
            o_ref[0, rows, hs] = (acc * (1.0 / l)).astype(o_ref.dtype)
            lse_tile = jnp.where(col == h, m + jnp.log(l), lse_tile)
        lse_ref[0, rows, :] = lse_tile


def _attention_group(proj, gi, dilation, tq=512):
    b, s, _ = proj.shape
    sub = s // dilation
    tq = min(tq, sub)
    nsub = tq // ATTN_BLK
    pv = proj.reshape(b, sub, dilation * N_IN)
    cb = N_IN // ATTN_WIDTH
    base = gi * 3

    def col(which):
        return lambda bi, r, n: (bi, n, r * cb + base + which)

    def col_prev(which):
        return lambda bi, r, n: (bi, jnp.maximum(n * nsub - 1, 0), r * cb + base + which)

    o, lse = pl.pallas_call(
        functools.partial(_attn_kernel, tq=tq),
        grid=(b, dilation, sub // tq),
        in_specs=[pl.BlockSpec((1, tq, ATTN_WIDTH), col(0)),
                  pl.BlockSpec((1, tq, ATTN_WIDTH), col(1)),
                  pl.BlockSpec((1, tq, ATTN_WIDTH), col(2)),
                  pl.BlockSpec((1, ATTN_BLK, ATTN_WIDTH), col_prev(1)),
                  pl.BlockSpec((1, ATTN_BLK, ATTN_WIDTH), col_prev(2))],
        out_specs=[pl.BlockSpec((1, tq, ATTN_WIDTH), lambda bi, r, n: (bi, n, r)),
                   pl.BlockSpec((1, tq, LANES), lambda bi, r, n: (bi, n, r))],
        out_shape=[jax.ShapeDtypeStruct((b, sub, dilation * ATTN_WIDTH), BF16),
                   jax.ShapeDtypeStruct((b, sub, dilation * LANES), F32)],
        compiler_params=_params(("arbitrary", "arbitrary", "arbitrary")),
        name=f"attn_d{dilation}",
    )(pv, pv, pv, pv, pv)
    return o.reshape(b * s, ATTN_WIDTH), lse.reshape(b * s, LANES)


def _ret_kernel(q_ref, k_ref, v_ref, g_ref, cos_ref, sin_ref, intra_ref, qd_ref, kd_ref, cd_ref,
                gng_ref, gnb_ref, o_ref, state_ref, *, rows):
    @pl.when(pl.program_id(2) == 0)
    def _():
        state_ref[...] = jnp.zeros_like(state_ref)

    half = RET_KEY_DIM // 2
    for c in range(rows // RET_CHUNK):
        rs = slice(c * RET_CHUNK, (c + 1) * RET_CHUNK)
        cos = cos_ref[0, rs, :]
        sin = sin_ref[0, rs, :]

        def rot(t):
            t = t.astype(F32)
            t1, t2 = t[:, :half], t[:, half:]
            return jnp.concatenate([t1 * cos - t2 * sin, t1 * sin + t2 * cos], axis=1)

        qr = rot(q_ref[0, rs, :]).astype(BF16)
        kr = (rot(k_ref[0, rs, :]) * (RET_KEY_DIM ** -0.5)).astype(BF16)
        v = v_ref[0, rs, :]
        a = _dot_nt(qr, kr) * intra_ref[0]
        inner = _dot(a.astype(BF16), v)
        st = state_ref[...]
        cross = _dot(qr, st.astype(BF16)) * qd_ref[0]
        y = inner + cross
        vk = (v.astype(F32) * kd_ref[0]).astype(BF16)
        state_ref[...] = cd_ref[0] * st + _dot_tn(kr, vk)
        mu = jnp.mean(y, axis=-1, keepdims=True)
        d = y - mu
        var = jnp.mean(d * d, axis=-1, keepdims=True)
        yn = d * lax.rsqrt(var + GN_EPS) * gng_ref[...] + gnb_ref[...]
        g = g_ref[0, rs, :].astype(F32)
        o_ref[0, rs, :] = ((g * _sigmoid(g)) * yn).astype(o_ref.dtype)


def _retention_decays():
    c = RET_CHUNK
    log_gamma = jnp.log1p(-jnp.exp2(-5.0 - jnp.arange(RET_HEADS, dtype=F32)))
    idx = jnp.arange(c, dtype=F32)
    rel = idx[:, None] - idx[None, :]
    intra = jnp.where(rel >= 0, jnp.exp(log_gamma[:, None, None] * jnp.maximum(rel, 0.0)), 0.0)
    qd = jnp.exp(log_gamma[:, None] * (idx + 1.0))[:, :, None]
    kd = jnp.exp(log_gamma[:, None] * (c - 1.0 - idx))[:, :, None]
    cd = jnp.exp(log_gamma * c)[:, None, None]
    qd = jnp.broadcast_to(qd, (RET_HEADS, c, RET_VAL_DIM))
    kd = jnp.broadcast_to(kd, (RET_HEADS, c, RET_VAL_DIM))
    cd = jnp.broadcast_to(cd, (RET_HEADS, 1, RET_VAL_DIM))
    return intra, qd, kd, cd


def _retention(proj, cos, sin, decays, gn_g, gn_b, rows=512):
    b, s, _ = proj.shape
    intra, qd, kd, cd = decays
    qb, kb = COL_RET_Q // RET_KEY_DIM, COL_RET_K // RET_KEY_DIM
    vb, gb = COL_RET_V // RET_VAL_DIM, COL_RET_G // RET_VAL_DIM
    half = RET_KEY_DIM // 2
    c = RET_CHUNK
    out = pl.pallas_call(
        functools.partial(_ret_kernel, rows=rows),
        grid=(b, RET_HEADS, s // rows),
        in_specs=[pl.BlockSpec((1, rows, RET_KEY_DIM), lambda bi, h, n: (bi, n, qb + h)),
                  pl.BlockSpec((1, rows, RET_KEY_DIM), lambda bi, h, n: (bi, n, kb + h)),
                  pl.BlockSpec((1, rows, RET_VAL_DIM), lambda bi, h, n: (bi, n, vb + h)),
                  pl.BlockSpec((1, rows, RET_VAL_DIM), lambda bi, h, n: (bi, n, gb + h)),
                  pl.BlockSpec((1, rows, half), lambda bi, h, n: (bi, n, 0)),
                  pl.BlockSpec((1, rows, half), lambda bi, h, n: (bi, n, 0)),
                  pl.BlockSpec((1, c, c), lambda bi, h, n: (h, 0, 0)),
                  pl.BlockSpec((1, c, RET_VAL_DIM), lambda bi, h, n: (h, 0, 0)),
                  pl.BlockSpec((1, c, RET_VAL_DIM), lambda bi, h, n: (h, 0, 0)),
                  pl.BlockSpec((1, 1, RET_VAL_DIM), lambda bi, h, n: (h, 0, 0)),
                  pl.BlockSpec((1, RET_VAL_DIM), lambda bi, h, n: (0, h)),
                  pl.BlockSpec((1, RET_VAL_DIM), lambda bi, h, n: (0, h))],
        out_specs=pl.BlockSpec((1, rows, RET_VAL_DIM), lambda bi, h, n: (bi, n, h)),
        out_shape=jax.ShapeDtypeStruct((b, s, RET_HEADS * RET_VAL_DIM), BF16),
        scratch_shapes=[pltpu.VMEM((RET_KEY_DIM, RET_VAL_DIM), F32)],
        compiler_params=_params(("arbitrary", "arbitrary", "arbitrary")),
        name="retention",
    )(proj, proj, proj, proj, cos, sin, intra, qd, kd, cd, gn_g.reshape(1, -1), gn_b.reshape(1, -1))
    return out.reshape(b * s, RET_HEADS * RET_VAL_DIM)


def _merge_kernel(o0_ref, o1_ref, o2_ref, l0_ref, l1_ref, l2_ref, ret_ref, ga_ref, gr_ref, x_ref, p_ref,
                  wa_ref, wr_ref, wo_ref, wpg_ref, wpp_ref, g1_ref, b1_ref, x1b_ref, resid_ref, attn_ref):
    l0, l1, l2 = l0_ref[...], l1_ref[...], l2_ref[...]
    m = jnp.maximum(jnp.maximum(l0, l1), l2)
    e0, e1, e2 = jnp.exp(l0 - m), jnp.exp(l1 - m), jnp.exp(l2 - m)
    inv = 1.0 / (e0 + e1 + e2)
    w0, w1, w2 = e0 * inv, e1 * inv, e2 * inv
    for h in range(ATTN_HEADS):
        hs = slice(h * HEAD_DIM, (h + 1) * HEAD_DIM)
        a = (w0[:, h:h + 1] * o0_ref[:, hs].astype(F32)
             + w1[:, h:h + 1] * o1_ref[:, hs].astype(F32)
             + w2[:, h:h + 1] * o2_ref[:, hs].astype(F32))
        attn_ref[:, hs] = a.astype(BF16)
    pa = _dot(attn_ref[...], wa_ref[...])
    pr = _dot(ret_ref[...], wr_ref[...])
    merged = _sigmoid(ga_ref[...].astype(F32)) * pa + _sigmoid(gr_ref[...].astype(F32)) * pr
    mix = _dot(merged.astype(BF16), wo_ref[...])
    x1 = _layer_norm(DN_ALPHA * x_ref[...] + mix, g1_ref[...], b1_ref[...])
    x1b = x1.astype(BF16)
    x1b_ref[...] = x1b
    ple = _sigmoid(_dot(x1b, wpg_ref[...])) * _dot(p_ref[...].astype(BF16), wpp_ref[...])
    resid_ref[...] = DN_ALPHA * x1 + ple


def _merge(outs, lses, ret, proj2d, x2d, p2d, wa, wr, wo, wpg, wpp, g1, b1, tm=256):
    t = x2d.shape[0]

    def rows(width, cb=0):
        return pl.BlockSpec((tm, width), lambda i: (i, cb))

    def whole(arr):
        return pl.BlockSpec(arr.shape, lambda i: (0, 0))

    g1 = g1.reshape(1, -1)
    b1 = b1.reshape(1, -1)
    return pl.pallas_call(
        _merge_kernel,
        grid=(t // tm,),
        in_specs=[rows(ATTN_WIDTH)] * 3 + [rows(LANES)] * 3
                 + [rows(RET_HEADS * RET_VAL_DIM),
                    rows(D_MODEL, COL_GATE_A // D_MODEL), rows(D_MODEL, COL_GATE_R // D_MODEL),
                    rows(D_MODEL), rows(PLE_DIM),
                    whole(wa), whole(wr), whole(wo), whole(wpg), whole(wpp), whole(g1), whole(b1)],
        out_specs=[rows(D_MODEL), rows(D_MODEL)],
        out_shape=[jax.ShapeDtypeStruct((t, D_MODEL), BF16), jax.ShapeDtypeStruct((t, D_MODEL), F32)],
        scratch_shapes=[pltpu.VMEM((tm, ATTN_WIDTH), BF16)],
        compiler_params=_params(("arbitrary",)),
        name="merge_ln1_ple",
    )(*outs, *lses, ret, proj2d, proj2d, x2d, p2d, wa, wr, wo, wpg, wpp, g1, b1)


FFN_HALO = 16


def _ffn_kernel(x_ref, halo_ref, wg_ref, wu_ref, cwg_ref, cwu_ref, cbg_ref, cbu_ref, wd_ref, resid_ref,
                g2_ref, b2_ref, o_ref, xcat_ref, hg_ref, hu_ref, acc_ref, *, tm, tiles_per_seq):
    i = pl.program_id(0)
    j = pl.program_id(1)

    @pl.when(j == 0)
    def _():
        first = (i % tiles_per_seq) == 0
        halo = halo_ref[...]
        xcat_ref[0:FFN_HALO, :] = jnp.where(first, jnp.zeros_like(halo), halo)
        xcat_ref[FFN_HALO:, :] = x_ref[...]
        acc_ref[...] = jnp.zeros_like(acc_ref)

    xc = xcat_ref[...]
    hg_ref[...] = _dot(xc, wg_ref[...])
    hu_ref[...] = _dot(xc, wu_ref[...])

    def conv(h_ref, cw_ref, cb_ref):
        return (cw_ref[2:3, :] * h_ref[FFN_HALO:FFN_HALO + tm, :]
                + cw_ref[1:2, :] * h_ref[FFN_HALO - 1:FFN_HALO - 1 + tm, :]
                + cw_ref[0:1, :] * h_ref[FFN_HALO - 2:FFN_HALO - 2 + tm, :]
                + cb_ref[...])

    gate = conv(hg_ref, cwg_ref, cbg_ref)
    up = conv(hu_ref, cwu_ref, cbu_ref)
    c0 = 0.7978845608028654
    gelu = 0.5 * gate * (1.0 + jnp.tanh(c0 * (gate + 0.044715 * (gate * gate * gate))))
    acc_ref[...] += _dot((gelu * up).astype(BF16), wd_ref[...])

    @pl.when(j == pl.num_programs(1) - 1)
    def _():
        o_ref[...] = _layer_norm(resid_ref[...] + acc_ref[...], g2_ref[...], b2_ref[...])


def _ffn(x1b, resid, w_up, conv_w, conv_b, w_down, g2, b2, seq, tm=1024, tf=256):
    t = x1b.shape[0]
    nf = D_FF // tf
    conv_b = conv_b.reshape(1, -1)
    g2 = g2.reshape(1, -1)
    b2 = b2.reshape(1, -1)
    hb = tm // FFN_HALO
    return pl.pallas_call(
        functools.partial(_ffn_kernel, tm=tm, tiles_per_seq=seq // tm),
        grid=(t // tm, nf),
        in_specs=[pl.BlockSpec((tm, D_MODEL), lambda i, j: (i, 0)),
                  pl.BlockSpec((FFN_HALO, D_MODEL), lambda i, j: (jnp.maximum(i * hb - 1, 0), 0)),
                  pl.BlockSpec((D_MODEL, tf), lambda i, j: (0, j)),
                  pl.BlockSpec((D_MODEL, tf), lambda i, j: (0, nf + j)),
                  pl.BlockSpec((3, tf), lambda i, j: (0, j)),
                  pl.BlockSpec((3, tf), lambda i, j: (0, nf + j)),
                  pl.BlockSpec((1, tf), lambda i, j: (0, j)),
                  pl.BlockSpec((1, tf), lambda i, j: (0, nf + j)),
                  pl.BlockSpec((tf, D_MODEL), lambda i, j: (j, 0)),
                  pl.BlockSpec((tm, D_MODEL), lambda i, j: (i, 0)),
                  pl.BlockSpec((1, D_MODEL), lambda i, j: (0, 0)),
                  pl.BlockSpec((1, D_MODEL), lambda i, j: (0, 0))],
        out_specs=pl.BlockSpec((tm, D_MODEL), lambda i, j: (i, 0)),
        out_shape=jax.ShapeDtypeStruct((t, D_MODEL), F32),
        scratch_shapes=[pltpu.VMEM((tm + FFN_HALO, D_MODEL), BF16),
                        pltpu.VMEM((tm + FFN_HALO, tf), F32),
                        pltpu.VMEM((tm + FFN_HALO, tf), F32),
                        pltpu.VMEM((tm, D_MODEL), F32)],
        compiler_params=_params(("arbitrary", "arbitrary")),
        name="conv_ffn_ln2",
    )(x1b, x1b, w_up, w_up, conv_w, conv_w, conv_b, conv_b, w_down, resid, g2, b2)


def kernel(x, p, positions, w_in, w_attn_proj, w_ret_proj, ret_gn_g, ret_gn_b, w_out, ln1_g, ln1_b, w_up,
           conv_w, conv_b, w_down, w_ple_gate, w_ple_proj, ln2_g, ln2_b):
    b, s, d = x.shape
    t = b * s
    cos, sin = _rope_tables(positions)
    decays = _retention_decays()
    x2d = x.reshape(t, d)
    for i in range(w_in.shape[0]):
        proj2d = _in_proj(x2d, w_in[i].astype(BF16))
        proj = proj2d.reshape(b, s, N_IN)
        outs, lses = [], []
        for gi, (_, dilation) in enumerate(ATTN_GROUPS):
            o, lse = _attention_group(proj, gi, dilation)
            outs.append(o)
            lses.append(lse)
        ret = _retention(proj, cos, sin, decays, ret_gn_g[i], ret_gn_b[i])
        x1b, resid = _merge(outs, lses, ret, proj2d, x2d, p[i].reshape(t, PLE_DIM),
                            w_attn_proj[i].astype(BF16), w_ret_proj[i].astype(BF16), w_out[i].astype(BF16),
                            w_ple_gate[i].astype(BF16), w_ple_proj[i].astype(BF16), ln1_g[i], ln1_b[i])
        x2d = _ffn(x1b, resid, w_up[i].astype(BF16), conv_w[i], conv_b[i], w_down[i].astype(BF16),
                   ln2_g[i], ln2_b[i], s)
    return x2d.reshape(b, s, d)
```

```python
import functools

import jax
import jax.numpy as jnp
from jax import lax
from jax.experimental import pallas as pl
from jax.experimental.pallas import tpu as pltpu

F32 = jnp.float32
BF16 = jnp.bfloat16

D_MODEL = 1024
ATTN_DILATIONS = (1, 4, 16)
ATTN_HEADS = 8
HEAD_DIM = 128
ATTN_WIDTH = ATTN_HEADS * HEAD_DIM
RET_HEADS = 4
RET_KEY_DIM = 256
RET_VAL_DIM = 512
RET_CHUNK = 128
ROPE_BASE = 10000.0
D_FF = 2816
PLE_DIM = 256
N_IN = 17408
DEPTH = 2
DN_ALPHA = (2 * DEPTH) ** 0.25
LN_EPS = 1e-5
GN_EPS = 1e-6

COL_RET_Q = 3 * 3 * ATTN_WIDTH
COL_RET_K = COL_RET_Q + RET_HEADS * RET_KEY_DIM
COL_RET_V = COL_RET_K + RET_HEADS * RET_KEY_DIM
COL_RET_G = COL_RET_V + RET_HEADS * RET_VAL_DIM
COL_GATE_A = COL_RET_G + RET_HEADS * RET_VAL_DIM
COL_GATE_R = COL_GATE_A + D_MODEL

LANES = 128
ATTN_BLK = 128
ATTN_TILE = ATTN_BLK * max(ATTN_DILATIONS)
VMEM_LIMIT = 56 * 1024 * 1024


def _params(sem, vmem=VMEM_LIMIT):
    return pltpu.CompilerParams(dimension_semantics=sem, vmem_limit_bytes=vmem)


def _dot(a, b):
    return jnp.dot(a, b, preferred_element_type=F32)


def _dot_nt(a, b):
    return lax.dot_general(a, b, (((1,), (1,)), ((), ())), preferred_element_type=F32)


def _dot_tn(a, b):
    return lax.dot_general(a, b, (((0,), (0,)), ((), ())), preferred_element_type=F32)


def _sigmoid(x):
    return 1.0 / (1.0 + jnp.exp(-x))


def _layer_norm(y, g, b):
    mu = jnp.mean(y, axis=-1, keepdims=True)
    d = y - mu
    var = jnp.mean(d * d, axis=-1, keepdims=True)
    return d * lax.rsqrt(var + LN_EPS) * g + b


def _rope_kernel(pos_ref, freq_ref, cos_ref, sin_ref):
    ang = pos_ref[...] * freq_ref[...]
    cos_ref[...] = jnp.cos(ang)
    sin_ref[...] = jnp.sin(ang)


def _rope_tables(positions):
    b, s = positions.shape
    half = RET_KEY_DIM // 2
    pos = jnp.broadcast_to(positions.astype(F32).reshape(b * s, 1), (b * s, half))
    freq = jnp.power(ROPE_BASE, -jnp.arange(half, dtype=F32) / half).reshape(1, half)
    rows = 2048
    spec = pl.BlockSpec((rows, half), lambda i: (i, 0))
    cos, sin = pl.pallas_call(
        _rope_kernel,
        grid=(b * s // rows,),
        in_specs=[spec, pl.BlockSpec((1, half), lambda i: (0, 0))],
        out_specs=[spec, spec],
        out_shape=[jax.ShapeDtypeStruct((b * s, half), F32)] * 2,
        compiler_params=_params(("arbitrary",)),
        name="rope_tables",
    )(pos, freq)
    return cos.reshape(b, s, half), sin.reshape(b, s, half)


def _proj_kernel(x_ref, w_ref, o_ref, xb_ref, xs_ref, *, tm, tn):
    j = pl.program_id(1)

    @pl.when(j == 0)
    def _():
        xb_ref[0] = x_ref[...].astype(BF16)
        for c in range(x_ref.shape[1] // LANES):
            cs = slice(c * LANES, (c + 1) * LANES)
            xs_ref[...] = x_ref[:, cs]
            for gi, dil in enumerate(ATTN_DILATIONS):
                if dil == 1:
                    continue
                n = tm // dil
                for r in range(dil):
                    xb_ref[gi, r * n:(r + 1) * n, cs] = xs_ref[pl.ds(r, n, stride=dil), :].astype(BF16)

    grp = (j * tn) // (3 * ATTN_WIDTH)
    sel = jnp.where(grp < len(ATTN_DILATIONS), grp, 0)
    o_ref[...] = _dot(xb_ref[sel], w_ref[...]).astype(o_ref.dtype)


def _in_proj(x2d, w_bf16, tm=ATTN_TILE, tn=512):
    t, d = x2d.shape
    n = w_bf16.shape[1]
    return pl.pallas_call(
        functools.partial(_proj_kernel, tm=tm, tn=tn),
        grid=(t // tm, n // tn),
        in_specs=[pl.BlockSpec((tm, d), lambda i, j: (i, 0)),
                  pl.BlockSpec((d, tn), lambda i, j: (0, j))],
        out_specs=pl.BlockSpec((tm, tn), lambda i, j: (i, j)),
        out_shape=jax.ShapeDtypeStruct((t, n), BF16),
        scratch_shapes=[pltpu.VMEM((len(ATTN_DILATIONS), tm, d), BF16), pltpu.VMEM((tm, LANES), F32)],
        compiler_params=_params(("arbitrary", "arbitrary")),
        name="in_proj",
    )(x2d, w_bf16)


def _attn_block(q, kcat, vcat, has_prev):
    s = _dot_nt(q, kcat) * (HEAD_DIM ** -0.5)
    row = lax.broadcasted_iota(jnp.int32, s.shape, 0)
    col = lax.broadcasted_iota(jnp.int32, s.shape, 1)
    dist = row + ATTN_BLK - col
    mask = jnp.logical_and(dist >= 0, dist <= ATTN_BLK)
    if has_prev is not None:
        mask = jnp.logical_and(mask, jnp.logical_or(col >= ATTN_BLK, has_prev))
    s = jnp.where(mask, s, -jnp.inf)
    m = jnp.max(s, axis=-1, keepdims=True)
    p = jnp.exp(s - m)
    l = jnp.sum(p, axis=-1, keepdims=True)
    o = _dot(p.astype(BF16), vcat) * (1.0 / l)
    return o, jnp.broadcast_to(m + jnp.log(l), o.shape)


def _attn_kernel(*refs):
    (q0, k0, v0, q1, k1, v1, q2, k2, v2, kp0, vp0) = refs[:11]
    n1 = ATTN_DILATIONS[1]
    kp1 = refs[11:11 + n1]
    vp1 = refs[11 + n1:11 + 2 * n1]
    kp2, vp2, o_ref, o_scr, l_scr = refs[11 + 2 * n1:]
    has_prev = pl.program_id(1) > 0
    blk = ATTN_BLK

    def rows(start):
        return slice(start, start + blk)

    def two(start):
        return slice(start, start + 2 * blk)

    def cat(a, b):
        return jnp.concatenate([a, b], axis=0)

    for n in range(ATTN_TILE // blk):
        if n == 0:
            o, l = _attn_block(q0[rows(0), :], cat(kp0[...], k0[rows(0), :]), cat(vp0[...], v0[rows(0), :]), has_prev)
        else:
            o, l = _attn_block(q0[rows(n * blk), :], k0[two((n - 1) * blk), :], v0[two((n - 1) * blk), :], None)
        o_scr[0, rows(n * blk), :] = o
        l_scr[0, rows(n * blk), :] = l
    for gi, (q, k, v) in ((1, (q1, k1, v1)), (2, (q2, k2, v2))):
        dil = ATTN_DILATIONS[gi]
        per = ATTN_TILE // dil
        for r in range(dil):
            for n in range(per // blk):
                base = r * per + n * blk
                if n == 0:
                    if gi == 1:
                        kp, vp = kp1[r][...], vp1[r][...]
                    else:
                        kp, vp = kp2[rows(base), :], vp2[rows(base), :]
                    o, l = _attn_block(q[rows(base), :], cat(kp, k[rows(base), :]), cat(vp, v[rows(base), :]), has_prev)
                else:
                    o, l = _attn_block(q[rows(base), :], k[two(base - blk), :], v[two(base - blk), :], None)
                dst = pl.ds(n * blk * dil + r, blk, stride=dil)
                o_scr[gi, dst, :] = o
                l_scr[gi, dst, :] = l
    for c in range(ATTN_TILE // blk):
        rs = rows(c * blk)
        l0, l1, l2 = l_scr[0, rs, :], l_scr[1, rs, :], l_scr[2, rs, :]
        m = jnp.maximum(jnp.maximum(l0, l1), l2)
        e0, e1, e2 = jnp.exp(l0 - m), jnp.exp(l1 - m), jnp.exp(l2 - m)
        inv = 1.0 / (e0 + e1 + e2)
        o = (e0 * inv) * o_scr[0, rs, :] + (e1 * inv) * o_scr[1, rs, :] + (e2 * inv) * o_scr[2, rs, :]
        o_ref[rs, :] = o.astype(o_ref.dtype)


def _attention(proj2d, seq):
    t = proj2d.shape[0]
    tiles_per_seq = seq // ATTN_TILE
    heads_per_group = 3 * ATTN_HEADS
    n1, n2 = ATTN_DILATIONS[1], ATTN_DILATIONS[2]
    blocks_per_tile = ATTN_TILE // ATTN_BLK

    def tile(bi, it):
        return bi * tiles_per_seq + it

    def prev_tile(bi, it):
        return bi * tiles_per_seq + jnp.maximum(it - 1, 0)

    def cur(gi, which):
        return pl.BlockSpec((ATTN_TILE, HEAD_DIM),
                            lambda bi, it, h: (tile(bi, it), gi * heads_per_group + which * ATTN_HEADS + h))

    def prev_blk(gi, which, blk_in_tile):
        return pl.BlockSpec((ATTN_BLK, HEAD_DIM),
                            lambda bi, it, h: (prev_tile(bi, it) * blocks_per_tile + blk_in_tile,
                                               gi * heads_per_group + which * ATTN_HEADS + h))

    def prev_whole(gi, which):
        return pl.BlockSpec((ATTN_TILE, HEAD_DIM),
                            lambda bi, it, h: (prev_tile(bi, it), gi * heads_per_group + which * ATTN_HEADS + h))

    per1 = ATTN_TILE // n1 // ATTN_BLK
    in_specs = [cur(g, w) for g in range(3) for w in range(3)]
    in_specs += [prev_blk(0, 1, blocks_per_tile - 1), prev_blk(0, 2, blocks_per_tile - 1)]
    in_specs += [prev_blk(1, 1, r * per1 + per1 - 1) for r in range(n1)]
    in_specs += [prev_blk(1, 2, r * per1 + per1 - 1) for r in range(n1)]
    in_specs += [prev_whole(2, 1), prev_whole(2, 2)]
    assert ATTN_TILE // n2 == ATTN_BLK
    return pl.pallas_call(
        _attn_kernel,
        grid=(t // seq, tiles_per_seq, ATTN_HEADS),
        in_specs=in_specs,
        out_specs=pl.BlockSpec((ATTN_TILE, HEAD_DIM), lambda bi, it, h: (tile(bi, it), h)),
        out_shape=jax.ShapeDtypeStruct((t, ATTN_WIDTH), BF16),
        scratch_shapes=[pltpu.VMEM((3, ATTN_TILE, HEAD_DIM), F32), pltpu.VMEM((3, ATTN_TILE, HEAD_DIM), F32)],
        compiler_params=_params(("arbitrary", "arbitrary", "arbitrary")),
        name="dilated_attention",
    )(*([proj2d] * len(in_specs)))


def _ret_kernel(q_ref, k_ref, v_ref, g_ref, cos_ref, sin_ref, intra_ref, qd_ref, kd_ref, cd_ref,
                gng_ref, gnb_ref, o_ref, state_ref, *, rows):
    @pl.when(pl.program_id(2) == 0)
    def _():
        state_ref[...] = jnp.zeros_like(state_ref)

    half = RET_KEY_DIM // 2
    for c in range(rows // RET_CHUNK):
        rs = slice(c * RET_CHUNK, (c + 1) * RET_CHUNK)
        cos = cos_ref[0, rs, :]
        sin = sin_ref[0, rs, :]

        def rot(t):
            t = t.astype(F32)
            t1, t2 = t[:, :half], t[:, half:]
            return jnp.concatenate([t1 * cos - t2 * sin, t1 * sin + t2 * cos], axis=1)

        qr = rot(q_ref[0, rs, :]).astype(BF16)
        kr = (rot(k_ref[0, rs, :]) * (RET_KEY_DIM ** -0.5)).astype(BF16)
        v = v_ref[0, rs, :]
        a = _dot_nt(qr, kr) * intra_ref[0]
        inner = _dot(a.astype(BF16), v)
        st = state_ref[...]
        cross = _dot(qr, st.astype(BF16)) * qd_ref[0]
        y = inner + cross
        vk = (v.astype(F32) * kd_ref[0]).astype(BF16)
        state_ref[...] = cd_ref[0] * st + _dot_tn(kr, vk)
        mu = jnp.mean(y, axis=-1, keepdims=True)
        d = y - mu
        var = jnp.mean(d * d, axis=-1, keepdims=True)
        yn = d * lax.rsqrt(var + GN_EPS) * gng_ref[...] + gnb_ref[...]
        g = g_ref[0, rs, :].astype(F32)
        o_ref[0, rs, :] = ((g * _sigmoid(g)) * yn).astype(o_ref.dtype)


def _retention_decays():
    c = RET_CHUNK
    log_gamma = jnp.log1p(-jnp.exp2(-5.0 - jnp.arange(RET_HEADS, dtype=F32)))
    idx = jnp.arange(c, dtype=F32)
    rel = idx[:, None] - idx[None, :]
    intra = jnp.where(rel >= 0, jnp.exp(log_gamma[:, None, None] * jnp.maximum(rel, 0.0)), 0.0)
    qd = jnp.exp(log_gamma[:, None] * (idx + 1.0))[:, :, None]
    kd = jnp.exp(log_gamma[:, None] * (c - 1.0 - idx))[:, :, None]
    cd = jnp.exp(log_gamma * c)[:, None, None]
    qd = jnp.broadcast_to(qd, (RET_HEADS, c, RET_VAL_DIM))
    kd = jnp.broadcast_to(kd, (RET_HEADS, c, RET_VAL_DIM))
    cd = jnp.broadcast_to(cd, (RET_HEADS, 1, RET_VAL_DIM))
    return intra, qd, kd, cd


def _retention(proj, cos, sin, decays, gn_g, gn_b, rows=512):
    b, s, _ = proj.shape
    intra, qd, kd, cd = decays
    qb, kb = COL_RET_Q // RET_KEY_DIM, COL_RET_K // RET_KEY_DIM
    vb, gb = COL_RET_V // RET_VAL_DIM, COL_RET_G // RET_VAL_DIM
    half = RET_KEY_DIM // 2
    c = RET_CHUNK
    out = pl.pallas_call(
        functools.partial(_ret_kernel, rows=rows),
        grid=(b, RET_HEADS, s // rows),
        in_specs=[pl.BlockSpec((1, rows, RET_KEY_DIM), lambda bi, h, n: (bi, n, qb + h)),
                  pl.BlockSpec((1, rows, RET_KEY_DIM), lambda bi, h, n: (bi, n, kb + h)),
                  pl.BlockSpec((1, rows, RET_VAL_DIM), lambda bi, h, n: (bi, n, vb + h)),
                  pl.BlockSpec((1, rows, RET_VAL_DIM), lambda bi, h, n: (bi, n, gb + h)),
                  pl.BlockSpec((1, rows, half), lambda bi, h, n: (bi, n, 0)),
                  pl.BlockSpec((1, rows, half), lambda bi, h, n: (bi, n, 0)),
                  pl.BlockSpec((1, c, c), lambda bi, h, n: (h, 0, 0)),
                  pl.BlockSpec((1, c, RET_VAL_DIM), lambda bi, h, n: (h, 0, 0)),
                  pl.BlockSpec((1, c, RET_VAL_DIM), lambda bi, h, n: (h, 0, 0)),
                  pl.BlockSpec((1, 1, RET_VAL_DIM), lambda bi, h, n: (h, 0, 0)),
                  pl.BlockSpec((1, RET_VAL_DIM), lambda bi, h, n: (0, h)),
                  pl.BlockSpec((1, RET_VAL_DIM), lambda bi, h, n: (0, h))],
        out_specs=pl.BlockSpec((1, rows, RET_VAL_DIM), lambda bi, h, n: (bi, n, h)),
        out_shape=jax.ShapeDtypeStruct((b, s, RET_HEADS * RET_VAL_DIM), BF16),
        scratch_shapes=[pltpu.VMEM((RET_KEY_DIM, RET_VAL_DIM), F32)],
        compiler_params=_params(("arbitrary", "arbitrary", "arbitrary")),
        name="retention",
    )(proj, proj, proj, proj, cos, sin, intra, qd, kd, cd, gn_g.reshape(1, -1), gn_b.reshape(1, -1))
    return out.reshape(b * s, RET_HEADS * RET_VAL_DIM)


def _merge_kernel(attn_ref, ret_ref, ga_ref, gr_ref, x_ref, p_ref,
                  wa_ref, wr_ref, wo_ref, wpg_ref, wpp_ref, g1_ref, b1_ref, x1b_ref, resid_ref):
    pa = _dot(attn_ref[...], wa_ref[...])
    pr = _dot(ret_ref[...], wr_ref[...])
    merged = _sigmoid(ga_ref[...].astype(F32)) * pa + _sigmoid(gr_ref[...].astype(F32)) * pr
    mix = _dot(merged.astype(BF16), wo_ref[...])
    x1 = _layer_norm(DN_ALPHA * x_ref[...] + mix, g1_ref[...], b1_ref[...])
    x1b = x1.astype(BF16)
    x1b_ref[...] = x1b
    ple = _sigmoid(_dot(x1b, wpg_ref[...])) * _dot(p_ref[...].astype(BF16), wpp_ref[...])
    resid_ref[...] = DN_ALPHA * x1 + ple


def _merge(attn, ret, proj2d, x2d, p2d, wa, wr, wo, wpg, wpp, g1, b1, tm=256):
    t = x2d.shape[0]

    def rows(width, cb=0):
        return pl.BlockSpec((tm, width), lambda i: (i, cb))

    def whole(arr):
        return pl.BlockSpec(arr.shape, lambda i: (0, 0))

    g1 = g1.reshape(1, -1)
    b1 = b1.reshape(1, -1)
    return pl.pallas_call(
        _merge_kernel,
        grid=(t // tm,),
        in_specs=[rows(ATTN_WIDTH), rows(RET_HEADS * RET_VAL_DIM),
                  rows(D_MODEL, COL_GATE_A // D_MODEL), rows(D_MODEL, COL_GATE_R // D_MODEL),
                  rows(D_MODEL), rows(PLE_DIM),
                  whole(wa), whole(wr), whole(wo), whole(wpg), whole(wpp), whole(g1), whole(b1)],
        out_specs=[rows(D_MODEL), rows(D_MODEL)],
        out_shape=[jax.ShapeDtypeStruct((t, D_MODEL), BF16), jax.ShapeDtypeStruct((t, D_MODEL), F32)],
        compiler_params=_params(("arbitrary",)),
        name="merge_ln1_ple",
    )(attn, ret, proj2d, proj2d, x2d, p2d, wa, wr, wo, wpg, wpp, g1, b1)


FFN_HALO = 16


def _ffn_kernel(x_ref, halo_ref, wg_ref, wu_ref, cwg_ref, cwu_ref, cbg_ref, cbu_ref, wd_ref, resid_ref,
                g2_ref, b2_ref, o_ref, xcat_ref, hg_ref, hu_ref, acc_ref, *, tm, tiles_per_seq):
    i = pl.program_id(0)
    j = pl.program_id(1)

    @pl.when(j == 0)
    def _():
        first = (i % tiles_per_seq) == 0
        halo = halo_ref[...]
        xcat_ref[0:FFN_HALO, :] = jnp.where(first, jnp.zeros_like(halo), halo)
        xcat_ref[FFN_HALO:, :] = x_ref[...]
        acc_ref[...] = jnp.zeros_like(acc_ref)

    xc = xcat_ref[...]
    hg_ref[...] = _dot(xc, wg_ref[...])
    hu_ref[...] = _dot(xc, wu_ref[...])

    def conv(h_ref, cw_ref, cb_ref):
        return (cw_ref[2:3, :] * h_ref[FFN_HALO:FFN_HALO + tm, :]
                + cw_ref[1:2, :] * h_ref[FFN_HALO - 1:FFN_HALO - 1 + tm, :]
                + cw_ref[0:1, :] * h_ref[FFN_HALO - 2:FFN_HALO - 2 + tm, :]
                + cb_ref[...])

    gate = conv(hg_ref, cwg_ref, cbg_ref)
    up = conv(hu_ref, cwu_ref, cbu_ref)
    c0 = 0.7978845608028654
    gelu = 0.5 * gate * (1.0 + jnp.tanh(c0 * (gate + 0.044715 * (gate * gate * gate))))
    acc_ref[...] += _dot((gelu * up).astype(BF16), wd_ref[...])

    @pl.when(j == pl.num_programs(1) - 1)
    def _():
        o_ref[...] = _layer_norm(resid_ref[...] + acc_ref[...], g2_ref[...], b2_ref[...])


def _ffn(x1b, resid, w_up, conv_w, conv_b, w_down, g2, b2, seq, tm=1024, tf=256):
    t = x1b.shape[0]
    nf = D_FF // tf
    conv_b = conv_b.reshape(1, -1)
    g2 = g2.reshape(1, -1)
    b2 = b2.reshape(1, -1)
    hb = tm // FFN_HALO
    return pl.pallas_call(
        functools.partial(_ffn_kernel, tm=tm, tiles_per_seq=seq // tm),
        grid=(t // tm, nf),
        in_specs=[pl.BlockSpec((tm, D_MODEL), lambda i, j: (i, 0)),
                  pl.BlockSpec((FFN_HALO, D_MODEL), lambda i, j: (jnp.maximum(i * hb - 1, 0), 0)),
                  pl.BlockSpec((D_MODEL, tf), lambda i, j: (0, j)),
                  pl.BlockSpec((D_MODEL, tf), lambda i, j: (0, nf + j)),
                  pl.BlockSpec((3, tf), lambda i, j: (0, j)),
                  pl.BlockSpec((3, tf), lambda i, j: (0, nf + j)),
                  pl.BlockSpec((1, tf), lambda i, j: (0, j)),
                  pl.BlockSpec((1, tf), lambda i, j: (0, nf + j)),
                  pl.BlockSpec((tf, D_MODEL), lambda i, j: (j, 0)),
                  pl.BlockSpec((tm, D_MODEL), lambda i, j: (i, 0)),
                  pl.BlockSpec((1, D_MODEL), lambda i, j: (0, 0)),
                  pl.BlockSpec((1, D_MODEL), lambda i, j: (0, 0))],
        out_specs=pl.BlockSpec((tm, D_MODEL), lambda i, j: (i, 0)),
        out_shape=jax.ShapeDtypeStruct((t, D_MODEL), F32),
        scratch_shapes=[pltpu.VMEM((tm + FFN_HALO, D_MODEL), BF16),
                        pltpu.VMEM((tm + FFN_HALO, tf), F32),
                        pltpu.VMEM((tm + FFN_HALO, tf), F32),
                        pltpu.VMEM((tm, D_MODEL), F32)],
        compiler_params=_params(("arbitrary", "arbitrary")),
        name="conv_ffn_ln2",
    )(x1b, x1b, w_up, w_up, conv_w, conv_w, conv_b, conv_b, w_down, resid, g2, b2)


def kernel(x, p, positions, w_in, w_attn_proj, w_ret_proj, ret_gn_g, ret_gn_b, w_out, ln1_g, ln1_b, w_up,
           conv_w, conv_b, w_down, w_ple_gate, w_ple_proj, ln2_g, ln2_b):
    b, s, d = x.shape
    t = b * s
    assert s % ATTN_TILE == 0
    cos, sin = _rope_tables(positions)
    decays = _retention_decays()
    x2d = x.reshape(t, d)
    for i in range(w_in.shape[0]):
        proj2d = _in_proj(x2d, w_in[i].astype(BF16))
        attn = _attention(proj2d, s)
        ret = _retention(proj2d.reshape(b, s, N_IN), cos, sin, decays, ret_gn_g[i], ret_gn_b[i])
        x1b, resid = _merge(attn, ret, proj2d, x2d, p[i].reshape(t, PLE_DIM),
                            w_attn_proj[i].astype(BF16), w_ret_proj[i].astype(BF16), w_out[i].astype(BF16),
                            w_ple_gate[i].astype(BF16), w_ple_proj[i].astype(BF16), ln1_g[i], ln1_b[i])
        x2d = _ffn(x1b, resid, w_up[i].astype(BF16), conv_w[i], conv_b[i], w_down[i].astype(BF16),
                   ln2_g[i], ln2_b[i], s)
    return x2d.reshape(b, s, d)
```

```python
import functools

import numpy as np
import jax
import jax.numpy as jnp
from jax import lax
from jax.experimental import pallas as pl
from jax.experimental.pallas import tpu as pltpu

F32 = jnp.float32
BF16 = jnp.bfloat16

D_MODEL = 1024
ATTN_DILATIONS = (1, 4, 16)
ATTN_HEADS = 8
HEAD_DIM = 128
ATTN_WIDTH = ATTN_HEADS * HEAD_DIM
RET_HEADS = 4
RET_KEY_DIM = 256
RET_VAL_DIM = 512
RET_CHUNK = 128
ROPE_BASE = 10000.0
D_FF = 2816
PLE_DIM = 256
N_IN = 17408
DEPTH = 2
DN_ALPHA = (2 * DEPTH) ** 0.25
LN_EPS = 1e-5
GN_EPS = 1e-6
LOG2E = 1.4426950408889634

COL_RET_Q = 3 * 3 * ATTN_WIDTH
COL_RET_K = COL_RET_Q + RET_HEADS * RET_KEY_DIM
COL_RET_V = COL_RET_K + RET_HEADS * RET_KEY_DIM
COL_RET_G = COL_RET_V + RET_HEADS * RET_VAL_DIM
COL_GATE_A = COL_RET_G + RET_HEADS * RET_VAL_DIM
COL_GATE_R = COL_GATE_A + D_MODEL

LANES = 128
ATTN_BLK = 128
ATTN_TILE = ATTN_BLK * max(ATTN_DILATIONS)
VMEM_LIMIT = 56 * 1024 * 1024


def _params(sem, vmem=VMEM_LIMIT, flags=None):
    return pltpu.CompilerParams(dimension_semantics=sem, vmem_limit_bytes=vmem, flags=flags)


def _resident(arr, layer=None):
    if layer is None:
        return pl.BlockSpec(arr.shape, lambda *_: (0,) * arr.ndim, pipeline_mode=pl.Buffered(1))
    return pl.BlockSpec((None,) + arr.shape[1:], lambda *_: (layer,) + (0,) * (arr.ndim - 1),
                        pipeline_mode=pl.Buffered(1))


def _dot(a, b):
    return jnp.dot(a, b, preferred_element_type=F32)


def _dot_nt(a, b):
    return lax.dot_general(a, b, (((1,), (1,)), ((), ())), preferred_element_type=F32)


def _dot_tn(a, b):
    return lax.dot_general(a, b, (((0,), (0,)), ((), ())), preferred_element_type=F32)


def _sigmoid(x):
    return 1.0 / (1.0 + jnp.exp(-x))


def _layer_norm(y, g, b):
    mu = jnp.mean(y, axis=-1, keepdims=True)
    d = y - mu
    var = jnp.mean(d * d, axis=-1, keepdims=True)
    return d * lax.rsqrt(var + LN_EPS) * g + b


def _rope_kernel(pos_ref, freq_ref, cos_ref, sin_ref):
    ang = pos_ref[...] * freq_ref[...]
    cos_ref[...] = jnp.cos(ang)
    sin_ref[...] = jnp.sin(ang)


def _rope_tables(positions):
    b, s = positions.shape
    half = RET_KEY_DIM // 2
    pos = jnp.broadcast_to(positions.astype(F32).reshape(b * s, 1), (b * s, half))
    freq = jnp.power(ROPE_BASE, -jnp.arange(half, dtype=F32) / half).reshape(1, half)
    rows = 2048
    spec = pl.BlockSpec((rows, half), lambda i: (i, 0))
    cos, sin = pl.pallas_call(
        _rope_kernel,
        grid=(b * s // rows,),
        in_specs=[spec, pl.BlockSpec((1, half), lambda i: (0, 0))],
        out_specs=[spec, spec],
        out_shape=[jax.ShapeDtypeStruct((b * s, half), F32)] * 2,
        compiler_params=_params(("arbitrary",)),
        name="rope_tables",
    )(pos, freq)
    return cos, sin


def _proj_kernel(x_ref, w_ref, cs_ref, cos_ref, sin_ref, o_ref, xb_ref, xs_ref, *, tm, tn):
    j = pl.program_id(1)

    @pl.when(j == 0)
    def _():
        xb_ref[0] = x_ref[...].astype(BF16)
        for c in range(x_ref.shape[1] // LANES):
            cs = slice(c * LANES, (c + 1) * LANES)
            xs_ref[...] = x_ref[:, cs]
            for gi, dil in enumerate(ATTN_DILATIONS):
                if dil == 1:
                    continue
                n = tm // dil
                for r in range(dil):
                    xb_ref[gi, r * n:(r + 1) * n, cs] = xs_ref[pl.ds(r, n, stride=dil), :].astype(BF16)

    grp = (j * tn) // (3 * ATTN_WIDTH)
    sel = jnp.where(grp < len(ATTN_DILATIONS), grp, 0)
    is_rope = jnp.logical_and(j >= COL_RET_Q // tn, j < COL_RET_V // tn)

    @pl.when(jnp.logical_not(is_rope))
    def _():
        o_ref[...] = (_dot(xb_ref[sel], w_ref[...].astype(BF16)) * cs_ref[...]).astype(o_ref.dtype)

    @pl.when(is_rope)
    def _():
        y = _dot(xb_ref[0], w_ref[...].astype(BF16)) * cs_ref[...]
        cos, sin = cos_ref[...], sin_ref[...]
        half = RET_KEY_DIM // 2
        for h0 in range(0, tn, RET_KEY_DIM):
            t1, t2 = y[:, h0:h0 + half], y[:, h0 + half:h0 + RET_KEY_DIM]
            o_ref[:, h0:h0 + half] = (t1 * cos - t2 * sin).astype(o_ref.dtype)
            o_ref[:, h0 + half:h0 + RET_KEY_DIM] = (t1 * sin + t2 * cos).astype(o_ref.dtype)


def _proj_col_scale():
    scale = np.ones((1, N_IN), np.float32)
    for g in range(len(ATTN_DILATIONS)):
        scale[:, g * 3 * ATTN_WIDTH:g * 3 * ATTN_WIDTH + ATTN_WIDTH] = (HEAD_DIM ** -0.5) * LOG2E
    scale[:, COL_RET_K:COL_RET_V] = RET_KEY_DIM ** -0.5
    return jnp.asarray(scale)


def _in_proj(x2d, w_in, layer, col_scale, cos, sin, tm=ATTN_TILE, tn=512):
    t, d = x2d.shape
    n = w_in.shape[2]
    half = RET_KEY_DIM // 2
    assert COL_RET_Q % tn == 0 and COL_RET_V % tn == 0 and tn % RET_KEY_DIM == 0
    return pl.pallas_call(
        functools.partial(_proj_kernel, tm=tm, tn=tn),
        grid=(t // tm, n // tn),
        in_specs=[pl.BlockSpec((tm, d), lambda i, j: (i, 0)),
                  pl.BlockSpec((None, d, tn), lambda i, j: (layer, 0, j)),
                  pl.BlockSpec((1, tn), lambda i, j: (0, j)),
                  pl.BlockSpec((tm, half), lambda i, j: (i, 0)),
                  pl.BlockSpec((tm, half), lambda i, j: (i, 0))],
        out_specs=pl.BlockSpec((tm, tn), lambda i, j: (i, j)),
        out_shape=jax.ShapeDtypeStruct((t, n), BF16),
        scratch_shapes=[pltpu.VMEM((len(ATTN_DILATIONS), tm, d), BF16), pltpu.VMEM((tm, LANES), F32)],
        compiler_params=_params(("arbitrary", "arbitrary")),
        name="in_proj",
    )(x2d, w_in, col_scale, cos, sin)


def _attn_kernel(*refs):
    (q0, k0, v0, q1, k1, v1, q2, k2, v2, kp0, vp0) = refs[:11]
    n1 = ATTN_DILATIONS[1]
    kp1 = refs[11:11 + n1]
    vp1 = refs[11 + n1:11 + 2 * n1]
    kp2, vp2, o_ref, acc_scr, m_scr, l_scr, out_scr = refs[11 + 2 * n1:]
    has_prev = pl.program_id(1) > 0
    blk = ATTN_BLK

    def rows(start):
        return slice(start, start + blk)

    row = lax.broadcasted_iota(jnp.int32, (blk, 2 * blk), 0)
    col = lax.broadcasted_iota(jnp.int32, (blk, 2 * blk), 1)
    dist = row + blk - col
    band = jnp.logical_and(dist >= 0, dist <= blk)
    bias_mid = jnp.where(band, 0.0, -jnp.inf)
    bias_first = jnp.where(jnp.logical_and(band, jnp.logical_or(col >= blk, has_prev)), 0.0, -jnp.inf)
    ones = jnp.ones((2 * blk, HEAD_DIM), BF16)

    blocks = []
    for n in range(ATTN_TILE // blk):
        blocks.append((q0, k0, v0, n * blk, (kp0, vp0, 0) if n == 0 else None, 0))
    for gi, (q, k, v) in ((1, (q1, k1, v1)), (2, (q2, k2, v2))):
        dil = ATTN_DILATIONS[gi]
        per = ATTN_TILE // dil
        for r in range(dil):
            for n in range(per // blk):
                base = r * per + n * blk
                prev = None
                if n == 0:
                    prev = (kp1[r], vp1[r], 0) if gi == 1 else (kp2, vp2, base)
                blocks.append((q, k, v, base, prev, gi))

    def prev_and_cur(b, which):
        base, prev = b[3], b[4]
        if prev is None:
            return b[which][base - blk:base + blk, :]
        return jnp.concatenate([prev[which - 1][rows(prev[2]), :], b[which][rows(base), :]], axis=0)

    def scores(b):
        return _dot_nt(b[0][rows(b[3]), :], prev_and_cur(b, 1))

    def softmax_pv(s, b):
        m = jnp.max(s, axis=-1, keepdims=True)
        p = jnp.exp2(s - m)
        pv = _dot(p.astype(BF16), jnp.concatenate([prev_and_cur(b, 2), ones], axis=1))
        return pv[:, :HEAD_DIM], pv[:, HEAD_DIM:], jnp.broadcast_to(m, (blk, HEAD_DIM))

    n_stored = len(blocks) - ATTN_DILATIONS[2]
    s_next = scores(blocks[0])
    for i, b in enumerate(blocks):
        s = s_next + (bias_mid if b[4] is None else bias_first)
        if i + 1 < len(blocks):
            s_next = scores(blocks[i + 1])
        acc, l, m = softmax_pv(s, b)
        if i < n_stored:
            gi = b[5]
            acc_scr[gi, rows(b[3]), :] = acc
            l_scr[gi, rows(b[3]), :] = l
            m_scr[gi, rows(b[3]), :] = m
            continue
        r = i - n_stored
        d1, d2 = ATTN_DILATIONS[1], ATTN_DILATIONS[2]
        src0 = pl.ds(r, blk, stride=d2)
        src1 = pl.ds((r % d1) * (ATTN_TILE // d1) + r // d1, blk, stride=d2 // d1)
        m0, m1 = m_scr[0, src0, :], m_scr[1, src1, :]
        mx = jnp.maximum(jnp.maximum(m0, m1), m)
        e0, e1, e2 = jnp.exp2(m0 - mx), jnp.exp2(m1 - mx), jnp.exp2(m - mx)
        num = e0 * acc_scr[0, src0, :] + e1 * acc_scr[1, src1, :] + e2 * acc
        den = e0 * l_scr[0, src0, :] + e1 * l_scr[1, src1, :] + e2 * l
        out_scr[src0, :] = num * (1.0 / den)
    o_ref[...] = out_scr[...].astype(o_ref.dtype)


def _attention(proj2d, seq):
    t = proj2d.shape[0]
    tiles_per_seq = seq // ATTN_TILE
    heads_per_group = 3 * ATTN_HEADS
    n1, n2 = ATTN_DILATIONS[1], ATTN_DILATIONS[2]
    blocks_per_tile = ATTN_TILE // ATTN_BLK

    def tile(bi, it):
        return bi * tiles_per_seq + it

    def prev_tile(bi, it):
        return bi * tiles_per_seq + jnp.maximum(it - 1, 0)

    def cur(gi, which):
        return pl.BlockSpec((ATTN_TILE, HEAD_DIM),
                            lambda bi, it, h: (tile(bi, it), gi * heads_per_group + which * ATTN_HEADS + h))

    def prev_blk(gi, which, blk_in_tile):
        return pl.BlockSpec((ATTN_BLK, HEAD_DIM),
                            lambda bi, it, h: (prev_tile(bi, it) * blocks_per_tile + blk_in_tile,
                                               gi * heads_per_group + which * ATTN_HEADS + h))

    def prev_whole(gi, which):
        return pl.BlockSpec((ATTN_TILE, HEAD_DIM),
                            lambda bi, it, h: (prev_tile(bi, it), gi * heads_per_group + which * ATTN_HEADS + h))

    per1 = ATTN_TILE // n1 // ATTN_BLK
    in_specs = [cur(g, w) for g in range(3) for w in range(3)]
    in_specs += [prev_blk(0, 1, blocks_per_tile - 1), prev_blk(0, 2, blocks_per_tile - 1)]
    in_specs += [prev_blk(1, 1, r * per1 + per1 - 1) for r in range(n1)]
    in_specs += [prev_blk(1, 2, r * per1 + per1 - 1) for r in range(n1)]
    in_specs += [prev_whole(2, 1), prev_whole(2, 2)]
    assert ATTN_TILE // n2 == ATTN_BLK
    return pl.pallas_call(
        _attn_kernel,
        grid=(t // seq, tiles_per_seq, ATTN_HEADS),
        in_specs=in_specs,
        out_specs=pl.BlockSpec((ATTN_TILE, HEAD_DIM), lambda bi, it, h: (tile(bi, it), h)),
        out_shape=jax.ShapeDtypeStruct((t, ATTN_WIDTH), BF16),
        scratch_shapes=[pltpu.VMEM((2, ATTN_TILE, HEAD_DIM), F32)] * 3 + [pltpu.VMEM((ATTN_TILE, HEAD_DIM), F32)],
        compiler_params=_params(("arbitrary", "arbitrary", "arbitrary")),
        name="dilated_attention",
    )(*([proj2d] * len(in_specs)))


def _ret_kernel(q_ref, k_ref, v_ref, g_ref, intra_ref, qd_ref, kd_ref, cd_ref, gng_ref, gnb_ref,
                o_ref, state_ref, *, rows):
    @pl.when(pl.program_id(2) == 0)
    def _():
        state_ref[...] = jnp.zeros_like(state_ref)

    n_chunks = rows // RET_CHUNK

    def chunk(c):
        return slice(c * RET_CHUNK, (c + 1) * RET_CHUNK)

    def independent(c):
        q, k = q_ref[0, chunk(c), :], k_ref[0, chunk(c), :]
        kd = (k.astype(F32) * kd_ref[0]).astype(BF16)
        return _dot_nt(q, k), _dot_tn(kd, v_ref[0, chunk(c), :])

    nxt = independent(0)
    state = state_ref[...]
    for c in range(n_chunks):
        scores, update = nxt
        cross = _dot(q_ref[0, chunk(c), :], state.astype(BF16))
        if c + 1 < n_chunks:
            nxt = independent(c + 1)
        inner = _dot((scores * intra_ref[0]).astype(BF16), v_ref[0, chunk(c), :])
        state = cd_ref[0] * state + update
        y = inner + cross * qd_ref[0]
        mu = jnp.mean(y, axis=-1, keepdims=True)
        d = y - mu
        var = jnp.mean(d * d, axis=-1, keepdims=True)
        yn = d * lax.rsqrt(var + GN_EPS) * gng_ref[...] + gnb_ref[...]
        g = g_ref[0, chunk(c), :].astype(F32)
        o_ref[0, chunk(c), :] = ((g * _sigmoid(g)) * yn).astype(o_ref.dtype)
    state_ref[...] = state


def _retention_decays():
    c = RET_CHUNK
    log_gamma = jnp.log1p(-jnp.exp2(-5.0 - jnp.arange(RET_HEADS, dtype=F32)))
    idx = jnp.arange(c, dtype=F32)
    rel = idx[:, None] - idx[None, :]
    intra = jnp.where(rel >= 0, jnp.exp(log_gamma[:, None, None] * jnp.maximum(rel, 0.0)), 0.0)
    qd = jnp.exp(log_gamma[:, None] * (idx + 1.0))[:, :, None]
    kd = jnp.exp(log_gamma[:, None] * (c - 1.0 - idx))[:, :, None]
    cd = jnp.exp(log_gamma * c)[:, None, None]
    qd = jnp.broadcast_to(qd, (RET_HEADS, c, RET_VAL_DIM))
    kd = jnp.broadcast_to(kd, (RET_HEADS, c, RET_KEY_DIM))
    cd = jnp.broadcast_to(cd, (RET_HEADS, 1, RET_VAL_DIM))
    return intra, qd, kd, cd


def _retention(proj, decays, gn_g, gn_b, layer, rows=512):
    b, s, _ = proj.shape
    intra, qd, kd, cd = decays
    qb, kb = COL_RET_Q // RET_KEY_DIM, COL_RET_K // RET_KEY_DIM
    vb, gb = COL_RET_V // RET_VAL_DIM, COL_RET_G // RET_VAL_DIM
    c = RET_CHUNK
    out = pl.pallas_call(
        functools.partial(_ret_kernel, rows=rows),
        grid=(b, RET_HEADS, s // rows),
        in_specs=[pl.BlockSpec((1, rows, RET_KEY_DIM), lambda bi, h, n: (bi, n, qb + h)),
                  pl.BlockSpec((1, rows, RET_KEY_DIM), lambda bi, h, n: (bi, n, kb + h)),
                  pl.BlockSpec((1, rows, RET_VAL_DIM), lambda bi, h, n: (bi, n, vb + h)),
                  pl.BlockSpec((1, rows, RET_VAL_DIM), lambda bi, h, n: (bi, n, gb + h)),
                  pl.BlockSpec((1, c, c), lambda bi, h, n: (h, 0, 0)),
                  pl.BlockSpec((1, c, RET_VAL_DIM), lambda bi, h, n: (h, 0, 0)),
                  pl.BlockSpec((1, c, RET_KEY_DIM), lambda bi, h, n: (h, 0, 0)),
                  pl.BlockSpec((1, 1, RET_VAL_DIM), lambda bi, h, n: (h, 0, 0)),
                  pl.BlockSpec((None, 1, RET_VAL_DIM), lambda bi, h, n: (layer, 0, h)),
                  pl.BlockSpec((None, 1, RET_VAL_DIM), lambda bi, h, n: (layer, 0, h))],
        out_specs=pl.BlockSpec((1, rows, RET_VAL_DIM), lambda bi, h, n: (bi, n, h)),
        out_shape=jax.ShapeDtypeStruct((b, s, RET_HEADS * RET_VAL_DIM), BF16),
        scratch_shapes=[pltpu.VMEM((RET_KEY_DIM, RET_VAL_DIM), F32)],
        compiler_params=_params(("arbitrary", "arbitrary", "arbitrary")),
        name="retention",
    )(proj, proj, proj, proj, intra, qd, kd, cd, gn_g, gn_b)
    return out.reshape(b * s, RET_HEADS * RET_VAL_DIM)


def _merge_kernel(attn_ref, ret_ref, ga_ref, gr_ref, x_ref, p_ref,
                  wa_ref, wr_ref, wo_ref, wpg_ref, wpp_ref, g1_ref, b1_ref, x1b_ref, resid_ref):
    pa = _dot(attn_ref[...], wa_ref[...])
    pr = _dot(ret_ref[...], wr_ref[...])
    merged = _sigmoid(ga_ref[...].astype(F32)) * pa + _sigmoid(gr_ref[...].astype(F32)) * pr
    mix = _dot(merged.astype(BF16), wo_ref[...])
    x1 = _layer_norm(DN_ALPHA * x_ref[...] + mix, g1_ref[...], b1_ref[...])
    x1b = x1.astype(BF16)
    x1b_ref[...] = x1b
    ple = _sigmoid(_dot(x1b, wpg_ref[...])) * _dot(p_ref[...].astype(BF16), wpp_ref[...])
    resid_ref[...] = DN_ALPHA * x1 + ple


def _merge(attn, ret, proj2d, x2d, p3d, wa, wr, wo, wpg, wpp, g1, b1, layer, tm=512):
    t = x2d.shape[0]

    def rows(width, cb=0):
        return pl.BlockSpec((tm, width), lambda i: (i, cb))

    return pl.pallas_call(
        _merge_kernel,
        grid=(t // tm,),
        in_specs=[rows(ATTN_WIDTH), rows(RET_HEADS * RET_VAL_DIM),
                  rows(D_MODEL, COL_GATE_A // D_MODEL), rows(D_MODEL, COL_GATE_R // D_MODEL),
                  rows(D_MODEL), pl.BlockSpec((None, tm, PLE_DIM), lambda i: (layer, i, 0)),
                  _resident(wa, layer), _resident(wr, layer), _resident(wo, layer), _resident(wpg, layer),
                  _resident(wpp, layer), _resident(g1, layer), _resident(b1, layer)],
        out_specs=[rows(D_MODEL), rows(D_MODEL)],
        out_shape=[jax.ShapeDtypeStruct((t, D_MODEL), BF16), jax.ShapeDtypeStruct((t, D_MODEL), F32)],
        compiler_params=_params(("arbitrary",)),
        name="merge_ln1_ple",
    )(attn, ret, proj2d, proj2d, x2d, p3d, wa, wr, wo, wpg, wpp, g1, b1)


FFN_HALO = 16


def _ffn_kernel(x_ref, halo_ref, wup_ref, cw_ref, cb_ref, wd_ref, resid_ref, g2_ref, b2_ref, o_ref,
                xcat_ref, *, tm, sub, tf, tiles_per_seq):
    first = (pl.program_id(0) % tiles_per_seq) == 0
    halo = halo_ref[...]
    xcat_ref[0:FFN_HALO, :] = jnp.where(first, jnp.zeros_like(halo), halo)
    xcat_ref[FFN_HALO:, :] = x_ref[...]

    c0 = 0.7978845608028654
    pairs = [(r, c) for r in range(tm // sub) for c in range(D_FF // tf)]

    def up_proj(k):
        r, c = pairs[k]
        xc = xcat_ref[r * sub:r * sub + sub + FFN_HALO, :]
        return (_dot(xc, wup_ref[:, c * tf:(c + 1) * tf]),
                _dot(xc, wup_ref[:, D_FF + c * tf:D_FF + (c + 1) * tf]))

    def conv(h, col0):
        cs = slice(col0, col0 + tf)
        return (cw_ref[2:3, cs] * h[FFN_HALO:FFN_HALO + sub, :]
                + cw_ref[1:2, cs] * h[FFN_HALO - 1:FFN_HALO - 1 + sub, :]
                + cw_ref[0:1, cs] * h[FFN_HALO - 2:FFN_HALO - 2 + sub, :]
                + cb_ref[:, cs])

    h_next = up_proj(0)
    acc = None
    for k, (r, c) in enumerate(pairs):
        hg, hu = h_next
        if k + 1 < len(pairs):
            h_next = up_proj(k + 1)
        gate = conv(hg, c * tf)
        up = conv(hu, D_FF + c * tf)
        gelu = 0.5 * gate * (1.0 + jnp.tanh(c0 * (gate + 0.044715 * (gate * gate * gate))))
        d = _dot((gelu * up).astype(BF16), wd_ref[c * tf:(c + 1) * tf, :])
        acc = d if c == 0 else acc + d
        if c == D_FF // tf - 1:
            rs = slice(r * sub, (r + 1) * sub)
            o_ref[rs, :] = _layer_norm(resid_ref[rs, :] + acc, g2_ref[...], b2_ref[...])


def _ffn(x1b, resid, w_up, conv_w, conv_b, w_down, g2, b2, layer, seq, tm=512, tf=256, sub=256):
    t = x1b.shape[0]
    hb = tm // FFN_HALO

    return pl.pallas_call(
        functools.partial(_ffn_kernel, tm=tm, sub=sub, tf=tf, tiles_per_seq=seq // tm),
        grid=(t // tm,),
        in_specs=[pl.BlockSpec((tm, D_MODEL), lambda i: (i, 0)),
                  pl.BlockSpec((FFN_HALO, D_MODEL), lambda i: (jnp.maximum(i * hb - 1, 0), 0)),
                  _resident(w_up, layer), _resident(conv_w, layer), _resident(conv_b, layer),
                  _resident(w_down, layer),
                  pl.BlockSpec((tm, D_MODEL), lambda i: (i, 0)),
                  _resident(g2, layer), _resident(b2, layer)],
        out_specs=pl.BlockSpec((tm, D_MODEL), lambda i: (i, 0)),
        out_shape=jax.ShapeDtypeStruct((t, D_MODEL), F32),
        scratch_shapes=[pltpu.VMEM((tm + FFN_HALO, D_MODEL), BF16)],
        compiler_params=_params(("arbitrary",)),
        name="conv_ffn_ln2",
    )(x1b, x1b, w_up, conv_w, conv_b, w_down, resid, g2, b2)


def kernel(x, p, positions, w_in, w_attn_proj, w_ret_proj, ret_gn_g, ret_gn_b, w_out, ln1_g, ln1_b, w_up,
           conv_w, conv_b, w_down, w_ple_gate, w_ple_proj, ln2_g, ln2_b):
    b, s, d = x.shape
    t = b * s
    assert s % ATTN_TILE == 0
    cos, sin = _rope_tables(positions)
    decays = _retention_decays()
    col_scale = _proj_col_scale()
    x2d = x.reshape(t, d)
    depth = w_in.shape[0]
    p3d = p.reshape(depth, t, PLE_DIM)
    row = lambda v: v.reshape(depth, 1, -1)
    wa, wr, wo = w_attn_proj.astype(BF16), w_ret_proj.astype(BF16), w_out.astype(BF16)
    wpg, wpp = w_ple_gate.astype(BF16), w_ple_proj.astype(BF16)
    wup, wdn = w_up.astype(BF16), w_down.astype(BF16)
    for i in range(depth):
        proj2d = _in_proj(x2d, w_in, i, col_scale, cos, sin)
        attn = _attention(proj2d, s)
        ret = _retention(proj2d.reshape(b, s, N_IN), decays, row(ret_gn_g), row(ret_gn_b), i)
        x1b, resid = _merge(attn, ret, proj2d, x2d, p3d, wa, wr, wo, wpg, wpp, row(ln1_g), row(ln1_b), i)
        x2d = _ffn(x1b, resid, wup, conv_w, row(conv_b), wdn, row(ln2_g), row(ln2_b), i, s)
    return x2d.reshape(b, s, d)
```

```python
import functools

import numpy as np
import jax
import jax.numpy as jnp
from jax import lax
from jax.experimental import pallas as pl
from jax.experimental.pallas import tpu as pltpu

F32 = jnp.float32
BF16 = jnp.bfloat16

D_MODEL = 1024
ATTN_DILATIONS = (1, 4, 16)
ATTN_HEADS = 8
HEAD_DIM = 128
ATTN_WIDTH = ATTN_HEADS * HEAD_DIM
RET_HEADS = 4
RET_KEY_DIM = 256
RET_VAL_DIM = 512
RET_CHUNK = 128
ROPE_BASE = 10000.0
D_FF = 2816
PLE_DIM = 256
N_IN = 17408
DEPTH = 2
DN_ALPHA = (2 * DEPTH) ** 0.25
LN_EPS = 1e-5
GN_EPS = 1e-6
LOG2E = 1.4426950408889634

COL_RET_Q = 3 * 3 * ATTN_WIDTH
COL_RET_K = COL_RET_Q + RET_HEADS * RET_KEY_DIM
COL_RET_V = COL_RET_K + RET_HEADS * RET_KEY_DIM
COL_RET_G = COL_RET_V + RET_HEADS * RET_VAL_DIM
COL_GATE_A = COL_RET_G + RET_HEADS * RET_VAL_DIM
COL_GATE_R = COL_GATE_A + D_MODEL

LANES = 128
ATTN_BLK = 128
ATTN_TILE = ATTN_BLK * max(ATTN_DILATIONS)
VMEM_LIMIT = 56 * 1024 * 1024


def _params(sem, vmem=VMEM_LIMIT, flags=None):
    return pltpu.CompilerParams(dimension_semantics=sem, vmem_limit_bytes=vmem, flags=flags)


def _resident(arr, layer=None):
    if layer is None:
        return pl.BlockSpec(arr.shape, lambda *_: (0,) * arr.ndim, pipeline_mode=pl.Buffered(1))
    return pl.BlockSpec((None,) + arr.shape[1:], lambda *_: (layer,) + (0,) * (arr.ndim - 1),
                        pipeline_mode=pl.Buffered(1))


def _dot(a, b):
    return jnp.dot(a, b, preferred_element_type=F32)


def _dot_nt(a, b):
    return lax.dot_general(a, b, (((1,), (1,)), ((), ())), preferred_element_type=F32)


def _dot_tn(a, b):
    return lax.dot_general(a, b, (((0,), (0,)), ((), ())), preferred_element_type=F32)


def _sigmoid(x):
    return 1.0 / (1.0 + jnp.exp2(x * -LOG2E))


def _layer_norm(y, g, b):
    mu = jnp.mean(y, axis=-1, keepdims=True)
    d = y - mu
    var = jnp.mean(d * d, axis=-1, keepdims=True)
    return d * lax.rsqrt(var + LN_EPS) * g + b


def _rope_kernel(pos_ref, freq_ref, cos_ref, sin_ref):
    ang = pos_ref[...] * freq_ref[...]
    cos_ref[...] = jnp.cos(ang)
    sin_ref[...] = jnp.sin(ang)


def _rope_tables(positions):
    b, s = positions.shape
    half = RET_KEY_DIM // 2
    pos = jnp.broadcast_to(positions.astype(F32).reshape(b * s, 1), (b * s, half))
    freq = jnp.power(ROPE_BASE, -jnp.arange(half, dtype=F32) / half).reshape(1, half)
    rows = 2048
    spec = pl.BlockSpec((rows, half), lambda i: (i, 0))
    cos, sin = pl.pallas_call(
        _rope_kernel,
        grid=(b * s // rows,),
        in_specs=[spec, pl.BlockSpec((1, half), lambda i: (0, 0))],
        out_specs=[spec, spec],
        out_shape=[jax.ShapeDtypeStruct((b * s, half), F32)] * 2,
        compiler_params=_params(("arbitrary",)),
        name="rope_tables",
    )(pos, freq)
    return cos, sin


PERM_BLK = 256


def _proj_kernel(x_ref, w_ref, cs_ref, cos_ref, sin_ref, perm_ref, o_ref, xb_ref, *, tm, tn):
    j = pl.program_id(1)

    @pl.when(j == 0)
    def _():
        xb_ref[0] = x_ref[...].astype(BF16)
        for blk in range(tm // PERM_BLK):
            xblk = xb_ref[0, blk * PERM_BLK:(blk + 1) * PERM_BLK, :]
            for gi, dil in enumerate(ATTN_DILATIONS):
                if dil == 1:
                    continue
                y = _dot(perm_ref[gi - 1], xblk).astype(BF16)
                n = PERM_BLK // dil
                for r in range(dil):
                    dst = r * (tm // dil) + blk * n
                    xb_ref[gi, dst:dst + n, :] = y[r * n:(r + 1) * n, :]

    grp = (j * tn) // (3 * ATTN_WIDTH)
    sel = jnp.where(grp < len(ATTN_DILATIONS), grp, 0)
    is_rope = jnp.logical_and(j >= COL_RET_Q // tn, j < COL_RET_V // tn)

    @pl.when(jnp.logical_not(is_rope))
    def _():
        o_ref[...] = (_dot(xb_ref[sel], w_ref[...].astype(BF16)) * cs_ref[...]).astype(o_ref.dtype)

    @pl.when(is_rope)
    def _():
        y = _dot(xb_ref[0], w_ref[...].astype(BF16)) * cs_ref[...]
        cos, sin = cos_ref[...], sin_ref[...]
        half = RET_KEY_DIM // 2
        for h0 in range(0, tn, RET_KEY_DIM):
            t1, t2 = y[:, h0:h0 + half], y[:, h0 + half:h0 + RET_KEY_DIM]
            o_ref[:, h0:h0 + half] = (t1 * cos - t2 * sin).astype(o_ref.dtype)
            o_ref[:, h0 + half:h0 + RET_KEY_DIM] = (t1 * sin + t2 * cos).astype(o_ref.dtype)


def _proj_col_scale():
    scale = np.ones((1, N_IN), np.float32)
    for g in range(len(ATTN_DILATIONS)):
        scale[:, g * 3 * ATTN_WIDTH:g * 3 * ATTN_WIDTH + ATTN_WIDTH] = (HEAD_DIM ** -0.5) * LOG2E
    scale[:, COL_RET_K:COL_RET_V] = RET_KEY_DIM ** -0.5
    return jnp.asarray(scale)


def _row_permutations():
    mats = []
    for dil in ATTN_DILATIONS[1:]:
        m = np.zeros((PERM_BLK, PERM_BLK), np.float32)
        u = np.arange(PERM_BLK)
        m[(u % dil) * (PERM_BLK // dil) + u // dil, u] = 1.0
        mats.append(m)
    return jnp.asarray(np.stack(mats), dtype=BF16)


def _in_proj(x2d, w_in, layer, col_scale, cos, sin, perms, tm=ATTN_TILE, tn=1024):
    t, d = x2d.shape
    n = w_in.shape[2]
    half = RET_KEY_DIM // 2
    assert COL_RET_Q % tn == 0 and COL_RET_V % tn == 0 and tn % RET_KEY_DIM == 0

    def per_row_tile(width):
        return pl.BlockSpec((tm, width), lambda i, j: (i, 0), pipeline_mode=pl.Buffered(1))

    return pl.pallas_call(
        functools.partial(_proj_kernel, tm=tm, tn=tn),
        grid=(t // tm, n // tn),
        in_specs=[per_row_tile(d),
                  pl.BlockSpec((None, d, tn), lambda i, j: (layer, 0, j)),
                  pl.BlockSpec((1, tn), lambda i, j: (0, j)),
                  per_row_tile(half), per_row_tile(half), _resident(perms)],
        out_specs=pl.BlockSpec((tm, tn), lambda i, j: (i, j)),
        out_shape=jax.ShapeDtypeStruct((t, n), BF16),
        scratch_shapes=[pltpu.VMEM((len(ATTN_DILATIONS), tm, d), BF16)],
        compiler_params=_params(("arbitrary", "arbitrary")),
        name="in_proj",
    )(x2d, w_in, col_scale, cos, sin, perms)


def _attn_kernel(*refs):
    (q0, k0, v0, q1, k1, v1, q2, k2, v2, kp0, vp0) = refs[:11]
    n1 = ATTN_DILATIONS[1]
    kp1 = refs[11:11 + n1]
    vp1 = refs[11 + n1:11 + 2 * n1]
    kp2, vp2, o_ref, acc_scr, m_scr, l_scr, out_scr = refs[11 + 2 * n1:]
    has_prev = pl.program_id(1) > 0
    blk = ATTN_BLK

    def rows(start):
        return slice(start, start + blk)

    row = lax.broadcasted_iota(jnp.int32, (blk, 2 * blk), 0)
    col = lax.broadcasted_iota(jnp.int32, (blk, 2 * blk), 1)
    dist = row + blk - col
    band = jnp.logical_and(dist >= 0, dist <= blk)
    bias_mid = jnp.where(band, 0.0, -jnp.inf)
    bias_first = jnp.where(jnp.logical_and(band, jnp.logical_or(col >= blk, has_prev)), 0.0, -jnp.inf)
    ones = jnp.ones((2 * blk, HEAD_DIM), BF16)

    blocks = []
    for n in range(ATTN_TILE // blk):
        blocks.append((q0, k0, v0, n * blk, (kp0, vp0, 0) if n == 0 else None, 0))
    for gi, (q, k, v) in ((1, (q1, k1, v1)), (2, (q2, k2, v2))):
        dil = ATTN_DILATIONS[gi]
        per = ATTN_TILE // dil
        for r in range(dil):
            for n in range(per // blk):
                base = r * per + n * blk
                prev = None
                if n == 0:
                    prev = (kp1[r], vp1[r], 0) if gi == 1 else (kp2, vp2, base)
                blocks.append((q, k, v, base, prev, gi))

    def prev_and_cur(b, which):
        base, prev = b[3], b[4]
        if prev is None:
            return b[which][base - blk:base + blk, :]
        return jnp.concatenate([prev[which - 1][rows(prev[2]), :], b[which][rows(base), :]], axis=0)

    def scores(b):
        return _dot_nt(b[0][rows(b[3]), :], prev_and_cur(b, 1))

    def softmax_pv(s, b):
        m = jnp.max(s, axis=-1, keepdims=True)
        p = jnp.exp2(s - m)
        pv = _dot(p.astype(BF16), jnp.concatenate([prev_and_cur(b, 2), ones], axis=1))
        return pv[:, :HEAD_DIM], pv[:, HEAD_DIM:], jnp.broadcast_to(m, (blk, HEAD_DIM))

    n_stored = len(blocks) - ATTN_DILATIONS[2]
    ahead = 1
    s_queue = [scores(blocks[i]) for i in range(ahead)]
    for i, b in enumerate(blocks):
        s = s_queue.pop(0) + (bias_mid if b[4] is None else bias_first)
        if i + ahead < len(blocks):
            s_queue.append(scores(blocks[i + ahead]))
        acc, l, m = softmax_pv(s, b)
        if i < n_stored:
            gi = b[5]
            acc_scr[gi, rows(b[3]), :] = acc
            l_scr[gi, rows(b[3]), :] = l
            m_scr[gi, rows(b[3]), :] = m
            continue
        r = i - n_stored
        d1, d2 = ATTN_DILATIONS[1], ATTN_DILATIONS[2]
        src0 = pl.ds(r, blk, stride=d2)
        src1 = pl.ds((r % d1) * (ATTN_TILE // d1) + r // d1, blk, stride=d2 // d1)
        m0, m1 = m_scr[0, src0, :], m_scr[1, src1, :]
        mx = jnp.maximum(jnp.maximum(m0, m1), m)
        e0, e1, e2 = jnp.exp2(m0 - mx), jnp.exp2(m1 - mx), jnp.exp2(m - mx)
        num = e0 * acc_scr[0, src0, :] + e1 * acc_scr[1, src1, :] + e2 * acc
        den = e0 * l_scr[0, src0, :] + e1 * l_scr[1, src1, :] + e2 * l
        out_scr[src0, :] = num * (1.0 / den)
    o_ref[...] = out_scr[...].astype(o_ref.dtype)


def _attention(proj2d, seq):
    t = proj2d.shape[0]
    tiles_per_seq = seq // ATTN_TILE
    heads_per_group = 3 * ATTN_HEADS
    n1, n2 = ATTN_DILATIONS[1], ATTN_DILATIONS[2]
    blocks_per_tile = ATTN_TILE // ATTN_BLK

    def tile(bi, it):
        return bi * tiles_per_seq + it

    def prev_tile(bi, it):
        return bi * tiles_per_seq + jnp.maximum(it - 1, 0)

    def cur(gi, which):
        return pl.BlockSpec((ATTN_TILE, HEAD_DIM),
                            lambda bi, it, h: (tile(bi, it), gi * heads_per_group + which * ATTN_HEADS + h))

    def prev_blk(gi, which, blk_in_tile):
        return pl.BlockSpec((ATTN_BLK, HEAD_DIM),
                            lambda bi, it, h: (prev_tile(bi, it) * blocks_per_tile + blk_in_tile,
                                               gi * heads_per_group + which * ATTN_HEADS + h))

    def prev_whole(gi, which):
        return pl.BlockSpec((ATTN_TILE, HEAD_DIM),
                            lambda bi, it, h: (prev_tile(bi, it), gi * heads_per_group + which * ATTN_HEADS + h))

    per1 = ATTN_TILE // n1 // ATTN_BLK
    in_specs = [cur(g, w) for g in range(3) for w in range(3)]
    in_specs += [prev_blk(0, 1, blocks_per_tile - 1), prev_blk(0, 2, blocks_per_tile - 1)]
    in_specs += [prev_blk(1, 1, r * per1 + per1 - 1) for r in range(n1)]
    in_specs += [prev_blk(1, 2, r * per1 + per1 - 1) for r in range(n1)]
    in_specs += [prev_whole(2, 1), prev_whole(2, 2)]
    assert ATTN_TILE // n2 == ATTN_BLK
    return pl.pallas_call(
        _attn_kernel,
        grid=(t // seq, tiles_per_seq, ATTN_HEADS),
        in_specs=in_specs,
        out_specs=pl.BlockSpec((ATTN_TILE, HEAD_DIM), lambda bi, it, h: (tile(bi, it), h)),
        out_shape=jax.ShapeDtypeStruct((t, ATTN_WIDTH), BF16),
        scratch_shapes=[pltpu.VMEM((2, ATTN_TILE, HEAD_DIM), F32)] * 3 + [pltpu.VMEM((ATTN_TILE, HEAD_DIM), F32)],
        compiler_params=_params(("arbitrary", "arbitrary", "arbitrary")),
        name="dilated_attention",
    )(*([proj2d] * len(in_specs)))


def _ret_kernel(q_ref, k_ref, v_ref, g_ref, intra_ref, qd_ref, kd_ref, cd_ref, gng_ref, gnb_ref,
                o_ref, state_ref, *, rows):
    @pl.when(pl.program_id(2) == 0)
    def _():
        state_ref[...] = jnp.zeros_like(state_ref)

    n_chunks = rows // RET_CHUNK

    def chunk(c):
        return slice(c * RET_CHUNK, (c + 1) * RET_CHUNK)

    def independent(c):
        q, k = q_ref[0, chunk(c), :], k_ref[0, chunk(c), :]
        kd = (k.astype(F32) * kd_ref[0]).astype(BF16)
        return _dot_nt(q, k), _dot_tn(kd, v_ref[0, chunk(c), :])

    nxt = independent(0)
    state = state_ref[...]
    for c in range(n_chunks):
        scores, update = nxt
        qd = (q_ref[0, chunk(c), :].astype(F32) * qd_ref[0]).astype(BF16)
        y = _dot((scores * intra_ref[0]).astype(BF16), v_ref[0, chunk(c), :]) + _dot(qd, state.astype(BF16))
        if c + 1 < n_chunks:
            nxt = independent(c + 1)
        state = cd_ref[0] * state + update
        mu = jnp.mean(y, axis=-1, keepdims=True)
        d = y - mu
        var = jnp.mean(d * d, axis=-1, keepdims=True)
        yn = d * lax.rsqrt(var + GN_EPS) * gng_ref[...] + gnb_ref[...]
        g = g_ref[0, chunk(c), :].astype(F32)
        o_ref[0, chunk(c), :] = ((g * _sigmoid(g)) * yn).astype(o_ref.dtype)
    state_ref[...] = state


def _retention_decays():
    c = RET_CHUNK
    log_gamma = jnp.log1p(-jnp.exp2(-5.0 - jnp.arange(RET_HEADS, dtype=F32)))
    idx = jnp.arange(c, dtype=F32)
    rel = idx[:, None] - idx[None, :]
    intra = jnp.where(rel >= 0, jnp.exp(log_gamma[:, None, None] * jnp.maximum(rel, 0.0)), 0.0)
    qd = jnp.exp(log_gamma[:, None] * (idx + 1.0))[:, :, None]
    kd = jnp.exp(log_gamma[:, None] * (c - 1.0 - idx))[:, :, None]
    cd = jnp.exp(log_gamma * c)[:, None, None]
    qd = jnp.broadcast_to(qd, (RET_HEADS, c, RET_KEY_DIM))
    kd = jnp.broadcast_to(kd, (RET_HEADS, c, RET_KEY_DIM))
    cd = jnp.broadcast_to(cd, (RET_HEADS, 1, RET_VAL_DIM))
    return intra, qd, kd, cd


def _retention(proj, decays, gn_g, gn_b, layer, rows=1024):
    b, s, _ = proj.shape
    intra, qd, kd, cd = decays
    qb, kb = COL_RET_Q // RET_KEY_DIM, COL_RET_K // RET_KEY_DIM
    vb, gb = COL_RET_V // RET_VAL_DIM, COL_RET_G // RET_VAL_DIM
    c = RET_CHUNK
    out = pl.pallas_call(
        functools.partial(_ret_kernel, rows=rows),
        grid=(b, RET_HEADS, s // rows),
        in_specs=[pl.BlockSpec((1, rows, RET_KEY_DIM), lambda bi, h, n: (bi, n, qb + h)),
                  pl.BlockSpec((1, rows, RET_KEY_DIM), lambda bi, h, n: (bi, n, kb + h)),
                  pl.BlockSpec((1, rows, RET_VAL_DIM), lambda bi, h, n: (bi, n, vb + h)),
                  pl.BlockSpec((1, rows, RET_VAL_DIM), lambda bi, h, n: (bi, n, gb + h)),
                  pl.BlockSpec((1, c, c), lambda bi, h, n: (h, 0, 0)),
                  pl.BlockSpec((1, c, RET_KEY_DIM), lambda bi, h, n: (h, 0, 0)),
                  pl.BlockSpec((1, c, RET_KEY_DIM), lambda bi, h, n: (h, 0, 0)),
                  pl.BlockSpec((1, 1, RET_VAL_DIM), lambda bi, h, n: (h, 0, 0)),
                  pl.BlockSpec((None, 1, RET_VAL_DIM), lambda bi, h, n: (layer, 0, h)),
                  pl.BlockSpec((None, 1, RET_VAL_DIM), lambda bi, h, n: (layer, 0, h))],
        out_specs=pl.BlockSpec((1, rows, RET_VAL_DIM), lambda bi, h, n: (bi, n, h)),
        out_shape=jax.ShapeDtypeStruct((b, s, RET_HEADS * RET_VAL_DIM), BF16),
        scratch_shapes=[pltpu.VMEM((RET_KEY_DIM, RET_VAL_DIM), F32)],
        compiler_params=_params(("arbitrary", "arbitrary", "arbitrary")),
        name="retention",
    )(proj, proj, proj, proj, intra, qd, kd, cd, gn_g, gn_b)
    return out.reshape(b * s, RET_HEADS * RET_VAL_DIM)


def _merge_kernel(attn_ref, ret_ref, ga_ref, gr_ref, x_ref, p_ref,
                  wa_ref, wr_ref, wo_ref, wpg_ref, wpp_ref, g1_ref, b1_ref, x1b_ref, resid_ref, *, sub):
    blocks = [slice(r, r + sub) for r in range(0, attn_ref.shape[0], sub)]
    proj = [(_dot(attn_ref[rs, :], wa_ref[...]), _dot(ret_ref[rs, :], wr_ref[...]),
             _dot(p_ref[rs, :].astype(BF16), wpp_ref[...])) for rs in blocks]
    mix = []
    for rs, (pa, pr, _) in zip(blocks, proj):
        merged = _sigmoid(ga_ref[rs, :].astype(F32)) * pa + _sigmoid(gr_ref[rs, :].astype(F32)) * pr
        mix.append(_dot(merged.astype(BF16), wo_ref[...]))
    x1s, gates = [], []
    for rs, m in zip(blocks, mix):
        x1 = _layer_norm(DN_ALPHA * x_ref[rs, :] + m, g1_ref[...], b1_ref[...])
        x1b = x1.astype(BF16)
        x1b_ref[rs, :] = x1b
        x1s.append(x1)
        gates.append(_dot(x1b, wpg_ref[...]))
    for rs, x1, gate, (_, _, pp) in zip(blocks, x1s, gates, proj):
        resid_ref[rs, :] = DN_ALPHA * x1 + _sigmoid(gate) * pp


def _merge(attn, ret, proj2d, x2d, p3d, wa, wr, wo, wpg, wpp, g1, b1, layer, tm=512, sub=256):
    t = x2d.shape[0]

    def rows(width, cb=0):
        return pl.BlockSpec((tm, width), lambda i: (i, cb))

    return pl.pallas_call(
        functools.partial(_merge_kernel, sub=sub),
        grid=(t // tm,),
        in_specs=[rows(ATTN_WIDTH), rows(RET_HEADS * RET_VAL_DIM),
                  rows(D_MODEL, COL_GATE_A // D_MODEL), rows(D_MODEL, COL_GATE_R // D_MODEL),
                  rows(D_MODEL), pl.BlockSpec((None, tm, PLE_DIM), lambda i: (layer, i, 0)),
                  _resident(wa, layer), _resident(wr, layer), _resident(wo, layer), _resident(wpg, layer),
                  _resident(wpp, layer), _resident(g1, layer), _resident(b1, layer)],
        out_specs=[rows(D_MODEL), rows(D_MODEL)],
        out_shape=[jax.ShapeDtypeStruct((t, D_MODEL), BF16), jax.ShapeDtypeStruct((t, D_MODEL), F32)],
        compiler_params=_params(("arbitrary",)),
        name="merge_ln1_ple",
    )(attn, ret, proj2d, proj2d, x2d, p3d, wa, wr, wo, wpg, wpp, g1, b1)


FFN_HALO = 16


def _ffn_kernel(x_ref, halo_ref, wup_ref, cw_ref, cb_ref, wd_ref, resid_ref, g2_ref, b2_ref, o_ref,
                xcat_ref, *, tm, sub, tf, tiles_per_seq):
    first = (pl.program_id(0) % tiles_per_seq) == 0
    halo = halo_ref[...]
    xcat_ref[0:FFN_HALO, :] = jnp.where(first, jnp.zeros_like(halo), halo)
    xcat_ref[FFN_HALO:, :] = x_ref[...]

    c0 = 0.7978845608028654
    pairs = [(r, c) for r in range(tm // sub) for c in range(D_FF // tf)]

    def up_proj(k):
        r, c = pairs[k]
        xc = xcat_ref[r * sub:r * sub + sub + FFN_HALO, :]
        return (_dot(xc, wup_ref[:, c * tf:(c + 1) * tf]),
                _dot(xc, wup_ref[:, D_FF + c * tf:D_FF + (c + 1) * tf]))

    def conv(h, col0):
        cs = slice(col0, col0 + tf)
        return (cw_ref[2:3, cs] * h[FFN_HALO:FFN_HALO + sub, :]
                + cw_ref[1:2, cs] * h[FFN_HALO - 1:FFN_HALO - 1 + sub, :]
                + cw_ref[0:1, cs] * h[FFN_HALO - 2:FFN_HALO - 2 + sub, :]
                + cb_ref[:, cs])

    ahead = 2
    h_queue = [up_proj(k) for k in range(ahead)]
    acc = None
    for k, (r, c) in enumerate(pairs):
        hg, hu = h_queue.pop(0)
        if k + ahead < len(pairs):
            h_queue.append(up_proj(k + ahead))
        gate = conv(hg, c * tf)
        up = conv(hu, D_FF + c * tf)
        half_gate = 0.5 * gate
        gelu = half_gate + half_gate * jnp.tanh(gate * (c0 + (c0 * 0.044715) * (gate * gate)))
        d = _dot((gelu * up).astype(BF16), wd_ref[c * tf:(c + 1) * tf, :])
        acc = d if c == 0 else acc + d
        if c == D_FF // tf - 1:
            rs = slice(r * sub, (r + 1) * sub)
            o_ref[rs, :] = _layer_norm(resid_ref[rs, :] + acc, g2_ref[...], b2_ref[...])


def _ffn(x1b, resid, w_up, conv_w, conv_b, w_down, g2, b2, layer, seq, tm=512, tf=256, sub=256):
    t = x1b.shape[0]
    hb = tm // FFN_HALO

    return pl.pallas_call(
        functools.partial(_ffn_kernel, tm=tm, sub=sub, tf=tf, tiles_per_seq=seq // tm),
        grid=(t // tm,),
        in_specs=[pl.BlockSpec((tm, D_MODEL), lambda i: (i, 0)),
                  pl.BlockSpec((FFN_HALO, D_MODEL), lambda i: (jnp.maximum(i * hb - 1, 0), 0)),
                  _resident(w_up, layer), _resident(conv_w, layer), _resident(conv_b, layer),
                  _resident(w_down, layer),
                  pl.BlockSpec((tm, D_MODEL), lambda i: (i, 0)),
                  _resident(g2, layer), _resident(b2, layer)],
        out_specs=pl.BlockSpec((tm, D_MODEL), lambda i: (i, 0)),
        out_shape=jax.ShapeDtypeStruct((t, D_MODEL), F32),
        scratch_shapes=[pltpu.VMEM((tm + FFN_HALO, D_MODEL), BF16)],
        compiler_params=_params(("arbitrary",)),
        name="conv_ffn_ln2",
    )(x1b, x1b, w_up, conv_w, conv_b, w_down, resid, g2, b2)


def kernel(x, p, positions, w_in, w_attn_proj, w_ret_proj, ret_gn_g, ret_gn_b, w_out, ln1_g, ln1_b, w_up,
           conv_w, conv_b, w_down, w_ple_gate, w_ple_proj, ln2_g, ln2_b):
    b, s, d = x.shape
    t = b * s
    assert s % ATTN_TILE == 0
    cos, sin = _rope_tables(positions)
    decays = _retention_decays()
    col_scale = _proj_col_scale()
    perms = _row_permutations()
    x2d = x.reshape(t, d)
    depth = w_in.shape[0]
    p3d = p.reshape(depth, t, PLE_DIM)
    row = lambda v: v.reshape(depth, 1, -1)
    wa, wr, wo = w_attn_proj.astype(BF16), w_ret_proj.astype(BF16), w_out.astype(BF16)
    wpg, wpp = w_ple_gate.astype(BF16), w_ple_proj.astype(BF16)
    wup, wdn = w_up.astype(BF16), w_down.astype(BF16)
    for i in range(depth):
        proj2d = _in_proj(x2d, w_in, i, col_scale, cos, sin, perms)
        attn = _attention(proj2d, s)
        ret = _retention(proj2d.reshape(b, s, N_IN), decays, row(ret_gn_g), row(ret_gn_b), i)
        x1b, resid = _merge(attn, ret, proj2d, x2d, p3d, wa, wr, wo, wpg, wpp, row(ln1_g), row(ln1_b), i)
        x2d = _ffn(x1b, resid, wup, conv_w, row(conv_b), wdn, row(ln2_g), row(ln2_b), i, s)
    return x2d.reshape(b, s, d)
```

```python
import functools

import numpy as np
import jax
import jax.numpy as jnp
from jax import lax
from jax.experimental import pallas as pl
from jax.experimental.pallas import tpu as pltpu

F32 = jnp.float32
BF16 = jnp.bfloat16

D_MODEL = 1024
ATTN_DILATIONS = (1, 4, 16)
ATTN_HEADS = 8
HEAD_DIM = 128
ATTN_WIDTH = ATTN_HEADS * HEAD_DIM
RET_HEADS = 4
RET_KEY_DIM = 256
RET_VAL_DIM = 512
RET_CHUNK = 128
ROPE_BASE = 10000.0
D_FF = 2816
PLE_DIM = 256
N_IN = 17408
DEPTH = 2
DN_ALPHA = (2 * DEPTH) ** 0.25
LN_EPS = 1e-5
GN_EPS = 1e-6
LOG2E = 1.4426950408889634

COL_RET_Q = 3 * 3 * ATTN_WIDTH
COL_RET_K = COL_RET_Q + RET_HEADS * RET_KEY_DIM
COL_RET_V = COL_RET_K + RET_HEADS * RET_KEY_DIM
COL_RET_G = COL_RET_V + RET_HEADS * RET_VAL_DIM
COL_GATE_A = COL_RET_G + RET_HEADS * RET_VAL_DIM
COL_GATE_R = COL_GATE_A + D_MODEL

LANES = 128
ATTN_BLK = 128
ATTN_TILE = ATTN_BLK * max(ATTN_DILATIONS)
VMEM_LIMIT = 56 * 1024 * 1024


def _params(sem, vmem=VMEM_LIMIT, flags=None):
    return pltpu.CompilerParams(dimension_semantics=sem, vmem_limit_bytes=vmem, flags=flags)


def _resident(arr, layer=None):
    if layer is None:
        return pl.BlockSpec(arr.shape, lambda *_: (0,) * arr.ndim, pipeline_mode=pl.Buffered(1))
    return pl.BlockSpec((None,) + arr.shape[1:], lambda *_: (layer,) + (0,) * (arr.ndim - 1),
                        pipeline_mode=pl.Buffered(1))


def _dot(a, b):
    return jnp.dot(a, b, preferred_element_type=F32)


def _dot_nt(a, b):
    return lax.dot_general(a, b, (((1,), (1,)), ((), ())), preferred_element_type=F32)


def _dot_tn(a, b):
    return lax.dot_general(a, b, (((0,), (0,)), ((), ())), preferred_element_type=F32)


def _sigmoid(x):
    return 1.0 / (1.0 + jnp.exp2(x * -LOG2E))


def _layer_norm(y, g, b):
    mu = jnp.mean(y, axis=-1, keepdims=True)
    d = y - mu
    var = jnp.mean(d * d, axis=-1, keepdims=True)
    return d * lax.rsqrt(var + LN_EPS) * g + b


def _rope_kernel(pos_ref, freq_ref, cos_ref, sin_ref):
    ang = pos_ref[...] * freq_ref[...]
    cos_ref[...] = jnp.cos(ang)
    sin_ref[...] = jnp.sin(ang)


def _rope_tables(positions):
    b, s = positions.shape
    half = RET_KEY_DIM // 2
    pos = jnp.broadcast_to(positions.astype(F32).reshape(b * s, 1), (b * s, half))
    freq = jnp.power(ROPE_BASE, -jnp.arange(half, dtype=F32) / half).reshape(1, half)
    rows = 2048
    spec = pl.BlockSpec((rows, half), lambda i: (i, 0))
    cos, sin = pl.pallas_call(
        _rope_kernel,
        grid=(b * s // rows,),
        in_specs=[spec, pl.BlockSpec((1, half), lambda i: (0, 0))],
        out_specs=[spec, spec],
        out_shape=[jax.ShapeDtypeStruct((b * s, half), F32)] * 2,
        compiler_params=_params(("arbitrary",)),
        name="rope_tables",
    )(pos, freq)
    return cos, sin


PERM_BLK = 256


def _proj_kernel(x_ref, w_ref, cs_ref, cos_ref, sin_ref, perm_ref, o_ref, xb_ref, *, tm, tn):
    j = pl.program_id(1)

    @pl.when(j == 0)
    def _():
        xb_ref[0] = x_ref[...].astype(BF16)
        for blk in range(tm // PERM_BLK):
            xblk = xb_ref[0, blk * PERM_BLK:(blk + 1) * PERM_BLK, :]
            for gi, dil in enumerate(ATTN_DILATIONS):
                if dil == 1:
                    continue
                y = _dot(perm_ref[gi - 1], xblk).astype(BF16)
                n = PERM_BLK // dil
                for r in range(dil):
                    dst = r * (tm // dil) + blk * n
                    xb_ref[gi, dst:dst + n, :] = y[r * n:(r + 1) * n, :]

    grp = (j * tn) // (3 * ATTN_WIDTH)
    sel = jnp.where(grp < len(ATTN_DILATIONS), grp, 0)
    is_rope = jnp.logical_and(j >= COL_RET_Q // tn, j < COL_RET_V // tn)

    @pl.when(jnp.logical_not(is_rope))
    def _():
        o_ref[...] = (_dot(xb_ref[sel], w_ref[...].astype(BF16)) * cs_ref[...]).astype(o_ref.dtype)

    @pl.when(is_rope)
    def _():
        y = _dot(xb_ref[0], w_ref[...].astype(BF16)) * cs_ref[...]
        cos, sin = cos_ref[...], sin_ref[...]
        half = RET_KEY_DIM // 2
        for h0 in range(0, tn, RET_KEY_DIM):
            t1, t2 = y[:, h0:h0 + half], y[:, h0 + half:h0 + RET_KEY_DIM]
            o_ref[:, h0:h0 + half] = (t1 * cos - t2 * sin).astype(o_ref.dtype)
            o_ref[:, h0 + half:h0 + RET_KEY_DIM] = (t1 * sin + t2 * cos).astype(o_ref.dtype)


def _proj_col_scale():
    scale = np.ones((1, N_IN), np.float32)
    for g in range(len(ATTN_DILATIONS)):
        scale[:, g * 3 * ATTN_WIDTH:g * 3 * ATTN_WIDTH + ATTN_WIDTH] = (HEAD_DIM ** -0.5) * LOG2E
    scale[:, COL_RET_K:COL_RET_V] = RET_KEY_DIM ** -0.5
    return jnp.asarray(scale)


def _row_permutations():
    mats = []
    for dil in ATTN_DILATIONS[1:]:
        m = np.zeros((PERM_BLK, PERM_BLK), np.float32)
        u = np.arange(PERM_BLK)
        m[(u % dil) * (PERM_BLK // dil) + u // dil, u] = 1.0
        mats.append(m)
    return jnp.asarray(np.stack(mats), dtype=BF16)


def _in_proj(x2d, w_in, layer, col_scale, cos, sin, perms, tm=ATTN_TILE, tn=1024):
    t, d = x2d.shape
    n = w_in.shape[2]
    half = RET_KEY_DIM // 2
    assert COL_RET_Q % tn == 0 and COL_RET_V % tn == 0 and tn % RET_KEY_DIM == 0

    def per_row_tile(width):
        return pl.BlockSpec((tm, width), lambda i, j: (i, 0), pipeline_mode=pl.Buffered(1))

    return pl.pallas_call(
        functools.partial(_proj_kernel, tm=tm, tn=tn),
        grid=(t // tm, n // tn),
        in_specs=[pl.BlockSpec((tm, d), lambda i, j: (i, 0)),
                  pl.BlockSpec((None, d, tn), lambda i, j: (layer, 0, j)),
                  pl.BlockSpec((1, tn), lambda i, j: (0, j)),
                  per_row_tile(half), per_row_tile(half), _resident(perms)],
        out_specs=pl.BlockSpec((tm, tn), lambda i, j: (i, j)),
        out_shape=jax.ShapeDtypeStruct((t, n), BF16),
        scratch_shapes=[pltpu.VMEM((len(ATTN_DILATIONS), tm, d), BF16)],
        compiler_params=_params(("arbitrary", "arbitrary")),
        name="in_proj",
    )(x2d, w_in, col_scale, cos, sin, perms)


def _attn_blocks(refs, o_ref, o_scr, lse_scr, out_scr, has_prev):
    (q0, k0, v0, q1, k1, v1, q2, k2, v2, kp0, vp0) = refs[:11]
    n1 = ATTN_DILATIONS[1]
    kp1 = refs[11:11 + n1]
    vp1 = refs[11 + n1:11 + 2 * n1]
    kp2, vp2 = refs[11 + 2 * n1:]
    blk = ATTN_BLK

    def rows(start):
        return slice(start, start + blk)

    row = lax.broadcasted_iota(jnp.int32, (blk, 2 * blk), 0)
    col = lax.broadcasted_iota(jnp.int32, (blk, 2 * blk), 1)
    dist = row + blk - col
    band = jnp.logical_and(dist >= 0, dist <= blk)
    bias_mid = jnp.where(band, 0.0, -jnp.inf)
    bias_first = jnp.where(jnp.logical_and(band, jnp.logical_or(col >= blk, has_prev)), 0.0, -jnp.inf)
    ones = jnp.ones((2 * blk, HEAD_DIM), BF16)

    blocks = []
    for n in range(ATTN_TILE // blk):
        blocks.append((q0, k0, v0, n * blk, (kp0, vp0, 0) if n == 0 else None, 0))
    for gi, (q, k, v) in ((1, (q1, k1, v1)), (2, (q2, k2, v2))):
        dil = ATTN_DILATIONS[gi]
        per = ATTN_TILE // dil
        for r in range(dil):
            for n in range(per // blk):
                base = r * per + n * blk
                prev = None
                if n == 0:
                    prev = (kp1[r], vp1[r], 0) if gi == 1 else (kp2, vp2, base)
                blocks.append((q, k, v, base, prev, gi))

    def prev_and_cur(b, which):
        base, prev = b[3], b[4]
        if prev is None:
            return b[which][base - blk:base + blk, :]
        return jnp.concatenate([prev[which - 1][rows(prev[2]), :], b[which][rows(base), :]], axis=0)

    def scores(b):
        return _dot_nt(b[0][rows(b[3]), :], prev_and_cur(b, 1))

    def softmax_pv(s, b):
        m = jnp.max(s, axis=-1, keepdims=True)
        p = jnp.exp2(s - m)
        pv = _dot(p.astype(BF16), jnp.concatenate([prev_and_cur(b, 2), ones], axis=1))
        l = pv[:, HEAD_DIM:]
        return pv[:, :HEAD_DIM] * (1.0 / l), m + jnp.log2(l)

    n_stored = len(blocks) - ATTN_DILATIONS[2]
    ahead = 1
    s_queue = [scores(blocks[i]) for i in range(ahead)]
    for i, b in enumerate(blocks):
        s = s_queue.pop(0) + (bias_mid if b[4] is None else bias_first)
        if i + ahead < len(blocks):
            s_queue.append(scores(blocks[i + ahead]))
        o2, lse2 = softmax_pv(s, b)
        if i < n_stored:
            gi = b[5]
            o_scr[gi, rows(b[3]), :] = o2
            lse_scr[gi, rows(b[3]), :] = lse2
            yield
            continue
        r = i - n_stored
        d1, d2 = ATTN_DILATIONS[1], ATTN_DILATIONS[2]
        src0 = pl.ds(r, blk, stride=d2)
        src1 = pl.ds((r % d1) * (ATTN_TILE // d1) + r // d1, blk, stride=d2 // d1)
        lse0, lse1 = lse_scr[0, src0, :], lse_scr[1, src1, :]
        mx = jnp.maximum(jnp.maximum(lse0, lse1), lse2)
        e0, e1, e2 = jnp.exp2(lse0 - mx), jnp.exp2(lse1 - mx), jnp.exp2(lse2 - mx)
        num = e0 * o_scr[0, src0, :] + e1 * o_scr[1, src1, :] + e2 * o2
        out_scr[src0, :] = num * (1.0 / (e0 + e1 + e2))
        yield
    o_ref[...] = out_scr[...].astype(o_ref.dtype)


def _ret_chunks(q_ref, k_ref, v_ref, intra_ref, qd_ref, kd_ref, cd_ref, o_ref, state_ref, rows):
    n_chunks = rows // RET_CHUNK

    def chunk(c):
        return slice(c * RET_CHUNK, (c + 1) * RET_CHUNK)

    def independent(c):
        q, k = q_ref[0, chunk(c), :], k_ref[0, chunk(c), :]
        kd = (k.astype(F32) * kd_ref[0]).astype(BF16)
        return _dot_nt(q, k), _dot_tn(kd, v_ref[0, chunk(c), :])

    nxt = independent(0)
    state = state_ref[...]
    for c in range(n_chunks):
        scores, update = nxt
        qd = (q_ref[0, chunk(c), :].astype(F32) * qd_ref[0]).astype(BF16)
        y = _dot((scores * intra_ref[0]).astype(BF16), v_ref[0, chunk(c), :]) + _dot(qd, state.astype(BF16))
        if c + 1 < n_chunks:
            nxt = independent(c + 1)
        state = cd_ref[0] * state + update
        o_ref[0, chunk(c), :] = y.astype(o_ref.dtype)
        if c + 1 == n_chunks:
            state_ref[...] = state
        yield


def _mixer_kernel(*refs, n_attn_in, rows):
    attn_in = refs[:n_attn_in]
    ret_in = refs[n_attn_in:n_attn_in + 7]
    attn_out, ret_out, o_scr, lse_scr, out_scr, state_ref = refs[n_attn_in + 7:]
    n = pl.program_id(2)

    @pl.when(n == 0)
    def _():
        state_ref[...] = jnp.zeros_like(state_ref)

    steps_per_tile = ATTN_TILE // rows
    attn = _attn_blocks(attn_in, attn_out, o_scr, lse_scr, out_scr, n // steps_per_tile > 0)
    ret = _ret_chunks(*ret_in, ret_out, state_ref, rows)
    for _ in ret:
        pass
    for _ in attn:
        pass


def _token_mixers(proj2d, decays, batch, seq, rows=1024):
    t = proj2d.shape[0]
    intra, qd, kd, cd = decays
    tiles_per_seq = seq // ATTN_TILE
    steps_per_tile = ATTN_TILE // rows
    heads_per_step = ATTN_HEADS // RET_HEADS
    assert steps_per_tile == heads_per_step and seq % ATTN_TILE == 0
    heads_per_group = 3 * ATTN_HEADS
    n1, n2 = ATTN_DILATIONS[1], ATTN_DILATIONS[2]
    blocks_per_tile = ATTN_TILE // ATTN_BLK

    def tile(bi, n):
        return bi * tiles_per_seq + n // steps_per_tile

    def prev_tile(bi, n):
        return bi * tiles_per_seq + jnp.maximum(n // steps_per_tile - 1, 0)

    def head(hr, n):
        return (n % steps_per_tile) * RET_HEADS + hr

    def col(gi, which, hr, n):
        return gi * heads_per_group + which * ATTN_HEADS + head(hr, n)

    def cur(gi, which):
        return pl.BlockSpec((ATTN_TILE, HEAD_DIM), lambda bi, hr, n: (tile(bi, n), col(gi, which, hr, n)))

    def prev_blk(gi, which, blk_in_tile):
        return pl.BlockSpec((ATTN_BLK, HEAD_DIM),
                            lambda bi, hr, n: (prev_tile(bi, n) * blocks_per_tile + blk_in_tile,
                                               col(gi, which, hr, n)))

    def prev_whole(gi, which):
        return pl.BlockSpec((ATTN_TILE, HEAD_DIM), lambda bi, hr, n: (prev_tile(bi, n), col(gi, which, hr, n)))

    per1 = ATTN_TILE // n1 // ATTN_BLK
    attn_specs = [cur(g, w) for g in range(3) for w in range(3)]
    attn_specs += [prev_blk(0, 1, blocks_per_tile - 1), prev_blk(0, 2, blocks_per_tile - 1)]
    attn_specs += [prev_blk(1, 1, r * per1 + per1 - 1) for r in range(n1)]
    attn_specs += [prev_blk(1, 2, r * per1 + per1 - 1) for r in range(n1)]
    attn_specs += [prev_whole(2, 1), prev_whole(2, 2)]
    assert ATTN_TILE // n2 == ATTN_BLK

    steps_per_seq = seq // rows
    qb, kb, vb = COL_RET_Q // RET_KEY_DIM, COL_RET_K // RET_KEY_DIM, COL_RET_V // RET_VAL_DIM
    c = RET_CHUNK

    def ret_rows(width, cb):
        return pl.BlockSpec((1, rows, width), lambda bi, hr, n: (bi * steps_per_seq + n, 0, cb + hr))

    def per_head(shape):
        return pl.BlockSpec((1,) + shape, lambda bi, hr, n: (hr, 0, 0))

    proj3d = proj2d.reshape(t // rows, rows, N_IN)
    ret_specs = [ret_rows(RET_KEY_DIM, qb), ret_rows(RET_KEY_DIM, kb), ret_rows(RET_VAL_DIM, vb),
                 per_head((c, c)), per_head((c, RET_KEY_DIM)), per_head((c, RET_KEY_DIM)),
                 per_head((1, RET_VAL_DIM))]
    attn, ret = pl.pallas_call(
        functools.partial(_mixer_kernel, n_attn_in=len(attn_specs), rows=rows),
        grid=(batch, RET_HEADS, steps_per_seq),
        in_specs=attn_specs + ret_specs,
        out_specs=[pl.BlockSpec((ATTN_TILE, HEAD_DIM), lambda bi, hr, n: (tile(bi, n), head(hr, n))),
                   ret_rows(RET_VAL_DIM, 0)],
        out_shape=[jax.ShapeDtypeStruct((t, ATTN_WIDTH), BF16),
                   jax.ShapeDtypeStruct((t // rows, rows, RET_HEADS * RET_VAL_DIM), BF16)],
        scratch_shapes=[pltpu.VMEM((2, ATTN_TILE, HEAD_DIM), F32)] * 2
                       + [pltpu.VMEM((ATTN_TILE, HEAD_DIM), F32), pltpu.VMEM((RET_KEY_DIM, RET_VAL_DIM), F32)],
        compiler_params=_params(("arbitrary", "arbitrary", "arbitrary")),
        name="token_mixers",
    )(*([proj2d] * len(attn_specs)), proj3d, proj3d, proj3d, intra, qd, kd, cd)
    return attn, ret.reshape(t, RET_HEADS * RET_VAL_DIM)


def _retention_decays():
    c = RET_CHUNK
    log_gamma = jnp.log1p(-jnp.exp2(-5.0 - jnp.arange(RET_HEADS, dtype=F32)))
    idx = jnp.arange(c, dtype=F32)
    rel = idx[:, None] - idx[None, :]
    intra = jnp.where(rel >= 0, jnp.exp(log_gamma[:, None, None] * jnp.maximum(rel, 0.0)), 0.0)
    qd = jnp.exp(log_gamma[:, None] * (idx + 1.0))[:, :, None]
    kd = jnp.exp(log_gamma[:, None] * (c - 1.0 - idx))[:, :, None]
    cd = jnp.exp(log_gamma * c)[:, None, None]
    qd = jnp.broadcast_to(qd, (RET_HEADS, c, RET_KEY_DIM))
    kd = jnp.broadcast_to(kd, (RET_HEADS, c, RET_KEY_DIM))
    cd = jnp.broadcast_to(cd, (RET_HEADS, 1, RET_VAL_DIM))
    return intra, qd, kd, cd


def _merge_kernel(attn_ref, ret_ref, rg0_ref, rg1_ref, ga_ref, gr_ref, x_ref, p_ref, wa_ref, wr_ref, wo_ref,
                  wpg_ref, wpp_ref, gng_ref, gnb_ref, g1_ref, b1_ref, x1b_ref, resid_ref, *, sub):
    blocks = [slice(r, r + sub) for r in range(0, attn_ref.shape[0], sub)]

    def retention_proj(rs):
        pr = None
        for h in range(RET_HEADS):
            cs = slice(h * RET_VAL_DIM, (h + 1) * RET_VAL_DIM)
            y = ret_ref[rs, cs].astype(F32)
            mu = jnp.mean(y, axis=-1, keepdims=True)
            d = y - mu
            var = jnp.mean(d * d, axis=-1, keepdims=True)
            yn = d * lax.rsqrt(var + GN_EPS) * gng_ref[:, cs] + gnb_ref[:, cs]
            half = RET_HEADS // 2
            g_ref = rg0_ref if h < half else rg1_ref
            g = g_ref[rs, (h % half) * RET_VAL_DIM:(h % half + 1) * RET_VAL_DIM].astype(F32)
            part = _dot(((g * _sigmoid(g)) * yn).astype(BF16), wr_ref[cs, :])
            pr = part if pr is None else pr + part
        return pr

    first = [(_dot(attn_ref[rs, :], wa_ref[...]), _dot(p_ref[rs, :].astype(BF16), wpp_ref[...])) for rs in blocks]
    ret = [retention_proj(rs) for rs in blocks]
    mix = []
    for rs, (pa, _), pr in zip(blocks, first, ret):
        merged = _sigmoid(ga_ref[rs, :].astype(F32)) * pa + _sigmoid(gr_ref[rs, :].astype(F32)) * pr
        mix.append(_dot(merged.astype(BF16), wo_ref[...]))
    x1s, gates = [], []
    for rs, m in zip(blocks, mix):
        x1 = _layer_norm(DN_ALPHA * x_ref[rs, :] + m, g1_ref[...], b1_ref[...])
        x1b = x1.astype(BF16)
        x1b_ref[rs, :] = x1b
        x1s.append(x1)
        gates.append(_dot(x1b, wpg_ref[...]))
    for rs, x1, gate, (_, pp) in zip(blocks, x1s, gates, first):
        resid_ref[rs, :] = DN_ALPHA * x1 + _sigmoid(gate) * pp


def _merge(attn, ret, proj2d, x2d, p3d, wa, wr, wo, wpg, wpp, gn_g, gn_b, g1, b1, layer, tm=512, sub=256):
    t = x2d.shape[0]

    def rows(width, cb=0):
        return pl.BlockSpec((tm, width), lambda i: (i, cb))

    ret_width = RET_HEADS * RET_VAL_DIM
    gate_width = ret_width // 2
    assert COL_RET_G % gate_width == 0
    return pl.pallas_call(
        functools.partial(_merge_kernel, sub=sub),
        grid=(t // tm,),
        in_specs=[rows(ATTN_WIDTH), rows(ret_width),
                  rows(gate_width, COL_RET_G // gate_width), rows(gate_width, COL_RET_G // gate_width + 1),
                  rows(D_MODEL, COL_GATE_A // D_MODEL), rows(D_MODEL, COL_GATE_R // D_MODEL),
                  rows(D_MODEL), pl.BlockSpec((None, tm, PLE_DIM), lambda i: (layer, i, 0)),
                  _resident(wa, layer), _resident(wr, layer), _resident(wo, layer), _resident(wpg, layer),
                  _resident(wpp, layer), _resident(gn_g, layer), _resident(gn_b, layer),
                  _resident(g1, layer), _resident(b1, layer)],
        out_specs=[rows(D_MODEL), rows(D_MODEL)],
        out_shape=[jax.ShapeDtypeStruct((t, D_MODEL), BF16), jax.ShapeDtypeStruct((t, D_MODEL), F32)],
        compiler_params=_params(("arbitrary",)),
        name="merge_ln1_ple",
    )(attn, ret, proj2d, proj2d, proj2d, proj2d, x2d, p3d, wa, wr, wo, wpg, wpp, gn_g, gn_b, g1, b1)


FFN_HALO = 16


def _ffn_kernel(x_ref, halo_ref, wup_ref, cw_ref, cb_ref, wd_ref, resid_ref, g2_ref, b2_ref, o_ref,
                xcat_ref, *, tm, sub, tf, tiles_per_seq):
    first = (pl.program_id(0) % tiles_per_seq) == 0
    halo = halo_ref[...]
    xcat_ref[0:FFN_HALO, :] = jnp.where(first, jnp.zeros_like(halo), halo)
    xcat_ref[FFN_HALO:, :] = x_ref[...]

    c0 = 0.7978845608028654
    pairs = [(r, c) for r in range(tm // sub) for c in range(D_FF // tf)]

    def up_proj(k):
        r, c = pairs[k]
        xc = xcat_ref[r * sub:r * sub + sub + FFN_HALO, :]
        return (_dot(xc, wup_ref[:, c * tf:(c + 1) * tf]),
                _dot(xc, wup_ref[:, D_FF + c * tf:D_FF + (c + 1) * tf]))

    def conv(h, col0):
        cs = slice(col0, col0 + tf)
        return (cw_ref[2:3, cs] * h[FFN_HALO:FFN_HALO + sub, :]
                + cw_ref[1:2, cs] * h[FFN_HALO - 1:FFN_HALO - 1 + sub, :]
                + cw_ref[0:1, cs] * h[FFN_HALO - 2:FFN_HALO - 2 + sub, :]
                + cb_ref[:, cs])

    ahead = 2
    h_queue = [up_proj(k) for k in range(ahead)]
    acc = None
    for k, (r, c) in enumerate(pairs):
        hg, hu = h_queue.pop(0)
        if k + ahead < len(pairs):
            h_queue.append(up_proj(k + ahead))
        gate = conv(hg, c * tf)
        up = conv(hu, D_FF + c * tf)
        half_gate = 0.5 * gate
        gelu = half_gate + half_gate * jnp.tanh(gate * (c0 + (c0 * 0.044715) * (gate * gate)))
        d = _dot((gelu * up).astype(BF16), wd_ref[c * tf:(c + 1) * tf, :])
        acc = d if c == 0 else acc + d
        if c == D_FF // tf - 1:
            rs = slice(r * sub, (r + 1) * sub)
            o_ref[rs, :] = _layer_norm(resid_ref[rs, :] + acc, g2_ref[...], b2_ref[...])


def _ffn(x1b, resid, w_up, conv_w, conv_b, w_down, g2, b2, layer, seq, tm=512, tf=256, sub=256):
    t = x1b.shape[0]
    hb = tm // FFN_HALO

    return pl.pallas_call(
        functools.partial(_ffn_kernel, tm=tm, sub=sub, tf=tf, tiles_per_seq=seq // tm),
        grid=(t // tm,),
        in_specs=[pl.BlockSpec((tm, D_MODEL), lambda i: (i, 0)),
                  pl.BlockSpec((FFN_HALO, D_MODEL), lambda i: (jnp.maximum(i * hb - 1, 0), 0)),
                  _resident(w_up, layer), _resident(conv_w, layer), _resident(conv_b, layer),
                  _resident(w_down, layer),
                  pl.BlockSpec((tm, D_MODEL), lambda i: (i, 0)),
                  _resident(g2, layer), _resident(b2, layer)],
        out_specs=pl.BlockSpec((tm, D_MODEL), lambda i: (i, 0)),
        out_shape=jax.ShapeDtypeStruct((t, D_MODEL), F32),
        scratch_shapes=[pltpu.VMEM((tm + FFN_HALO, D_MODEL), BF16)],
        compiler_params=_params(("arbitrary",)),
        name="conv_ffn_ln2",
    )(x1b, x1b, w_up, conv_w, conv_b, w_down, resid, g2, b2)


def kernel(x, p, positions, w_in, w_attn_proj, w_ret_proj, ret_gn_g, ret_gn_b, w_out, ln1_g, ln1_b, w_up,
           conv_w, conv_b, w_down, w_ple_gate, w_ple_proj, ln2_g, ln2_b):
    b, s, d = x.shape
    t = b * s
    assert s % ATTN_TILE == 0
    cos, sin = _rope_tables(positions)
    decays = _retention_decays()
    col_scale = _proj_col_scale()
    perms = _row_permutations()
    x2d = x.reshape(t, d)
    depth = w_in.shape[0]
    p3d = p.reshape(depth, t, PLE_DIM)
    row = lambda v: v.reshape(depth, 1, -1)
    wa, wr, wo = w_attn_proj.astype(BF16), w_ret_proj.astype(BF16), w_out.astype(BF16)
    wpg, wpp = w_ple_gate.astype(BF16), w_ple_proj.astype(BF16)
    wup, wdn = w_up.astype(BF16), w_down.astype(BF16)
    for i in range(depth):
        proj2d = _in_proj(x2d, w_in, i, col_scale, cos, sin, perms)
        attn, ret = _token_mixers(proj2d, decays, b, s)
        x1b, resid = _merge(attn, ret, proj2d, x2d, p3d, wa, wr, wo, wpg, wpp, row(ret_gn_g), row(ret_gn_b),
                            row(ln1_g), row(ln1_b), i)
        x2d = _ffn(x1b, resid, wup, conv_w, row(conv_b), wdn, row(ln2_g), row(ln2_b), i, s)
    return x2d.reshape(b, s, d)
```

```python
import functools

import numpy as np
import jax
import jax.numpy as jnp
from jax import lax
from jax.experimental import pallas as pl
from jax.experimental.pallas import tpu as pltpu

F32 = jnp.float32
BF16 = jnp.bfloat16

D_MODEL = 1024
ATTN_DILATIONS = (1, 4, 16)
ATTN_HEADS = 8
HEAD_DIM = 128
ATTN_WIDTH = ATTN_HEADS * HEAD_DIM
RET_HEADS = 4
RET_KEY_DIM = 256
RET_VAL_DIM = 512
RET_CHUNK = 128
ROPE_BASE = 10000.0
D_FF = 2816
PLE_DIM = 256
N_IN = 17408
DEPTH = 2
DN_ALPHA = (2 * DEPTH) ** 0.25
LN_EPS = 1e-5
GN_EPS = 1e-6
LOG2E = 1.4426950408889634

COL_RET_Q = 3 * 3 * ATTN_WIDTH
COL_RET_K = COL_RET_Q + RET_HEADS * RET_KEY_DIM
COL_RET_V = COL_RET_K + RET_HEADS * RET_KEY_DIM
COL_RET_G = COL_RET_V + RET_HEADS * RET_VAL_DIM
COL_GATE_A = COL_RET_G + RET_HEADS * RET_VAL_DIM
COL_GATE_R = COL_GATE_A + D_MODEL

LANES = 128
ATTN_BLK = 128
ATTN_TILE = ATTN_BLK * max(ATTN_DILATIONS)
VMEM_LIMIT = 56 * 1024 * 1024


def _params(sem, vmem=VMEM_LIMIT, flags=None):
    return pltpu.CompilerParams(dimension_semantics=sem, vmem_limit_bytes=vmem, flags=flags)


def _resident(arr, layer=None):
    if layer is None:
        return pl.BlockSpec(arr.shape, lambda *_: (0,) * arr.ndim, pipeline_mode=pl.Buffered(1))
    return pl.BlockSpec((None,) + arr.shape[1:], lambda *_: (layer,) + (0,) * (arr.ndim - 1),
                        pipeline_mode=pl.Buffered(1))


def _dot(a, b):
    return jnp.dot(a, b, preferred_element_type=F32)


def _dot_nt(a, b):
    return lax.dot_general(a, b, (((1,), (1,)), ((), ())), preferred_element_type=F32)


def _dot_tn(a, b):
    return lax.dot_general(a, b, (((0,), (0,)), ((), ())), preferred_element_type=F32)


def _sigmoid(x):
    return 1.0 / (1.0 + jnp.exp2(x * -LOG2E))


def _layer_norm(y, g, b):
    mu = jnp.mean(y, axis=-1, keepdims=True)
    d = y - mu
    var = jnp.mean(d * d, axis=-1, keepdims=True)
    return d * lax.rsqrt(var + LN_EPS) * g + b


def _rope_kernel(pos_ref, freq_ref, cos_ref, sin_ref):
    ang = pos_ref[...] * freq_ref[...]
    cos_ref[...] = jnp.cos(ang)
    sin_ref[...] = jnp.sin(ang)


def _rope_tables(positions):
    b, s = positions.shape
    half = RET_KEY_DIM // 2
    pos = jnp.broadcast_to(positions.astype(F32).reshape(b * s, 1), (b * s, half))
    freq = jnp.power(ROPE_BASE, -jnp.arange(half, dtype=F32) / half).reshape(1, half)
    rows = 2048
    spec = pl.BlockSpec((rows, half), lambda i: (i, 0))
    cos, sin = pl.pallas_call(
        _rope_kernel,
        grid=(b * s // rows,),
        in_specs=[spec, pl.BlockSpec((1, half), lambda i: (0, 0))],
        out_specs=[spec, spec],
        out_shape=[jax.ShapeDtypeStruct((b * s, half), F32)] * 2,
        compiler_params=_params(("arbitrary",)),
        name="rope_tables",
    )(pos, freq)
    return cos, sin


PERM_BLK = 256


def _proj_kernel(x_ref, w_ref, cs_ref, cos_ref, sin_ref, perm_ref, o_ref, xb_ref, *, tm, tn):
    j = pl.program_id(1)

    @pl.when(j == 0)
    def _():
        xb_ref[0] = x_ref[...].astype(BF16)
        for blk in range(tm // PERM_BLK):
            xblk = xb_ref[0, blk * PERM_BLK:(blk + 1) * PERM_BLK, :]
            for gi, dil in enumerate(ATTN_DILATIONS):
                if dil == 1:
                    continue
                y = _dot(perm_ref[gi - 1], xblk).astype(BF16)
                n = PERM_BLK // dil
                for r in range(dil):
                    dst = r * (tm // dil) + blk * n
                    xb_ref[gi, dst:dst + n, :] = y[r * n:(r + 1) * n, :]

    grp = (j * tn) // (3 * ATTN_WIDTH)
    sel = jnp.where(grp < len(ATTN_DILATIONS), grp, 0)
    is_rope = jnp.logical_and(j >= COL_RET_Q // tn, j < COL_RET_V // tn)

    @pl.when(jnp.logical_not(is_rope))
    def _():
        o_ref[...] = (_dot(xb_ref[sel], w_ref[...].astype(BF16)) * cs_ref[...]).astype(o_ref.dtype)

    @pl.when(is_rope)
    def _():
        y = _dot(xb_ref[0], w_ref[...].astype(BF16)) * cs_ref[...]
        cos, sin = cos_ref[...], sin_ref[...]
        half = RET_KEY_DIM // 2
        for h0 in range(0, tn, RET_KEY_DIM):
            t1, t2 = y[:, h0:h0 + half], y[:, h0 + half:h0 + RET_KEY_DIM]
            o_ref[:, h0:h0 + half] = (t1 * cos - t2 * sin).astype(o_ref.dtype)
            o_ref[:, h0 + half:h0 + RET_KEY_DIM] = (t1 * sin + t2 * cos).astype(o_ref.dtype)


def _proj_col_scale():
    scale = np.ones((1, N_IN), np.float32)
    for g in range(len(ATTN_DILATIONS)):
        scale[:, g * 3 * ATTN_WIDTH:g * 3 * ATTN_WIDTH + ATTN_WIDTH] = (HEAD_DIM ** -0.5) * LOG2E
    scale[:, COL_RET_K:COL_RET_V] = RET_KEY_DIM ** -0.5
    return jnp.asarray(scale)


def _row_permutations():
    mats = []
    for dil in ATTN_DILATIONS[1:]:
        m = np.zeros((PERM_BLK, PERM_BLK), np.float32)
        u = np.arange(PERM_BLK)
        m[(u % dil) * (PERM_BLK // dil) + u // dil, u] = 1.0
        mats.append(m)
    return jnp.asarray(np.stack(mats), dtype=BF16)


def _in_proj(x2d, w_in, layer, col_scale, cos, sin, perms, tm=ATTN_TILE, tn=1024):
    t, d = x2d.shape
    n = w_in.shape[2]
    half = RET_KEY_DIM // 2
    assert COL_RET_Q % tn == 0 and COL_RET_V % tn == 0 and tn % RET_KEY_DIM == 0

    def per_row_tile(width):
        return pl.BlockSpec((tm, width), lambda i, j: (i, 0), pipeline_mode=pl.Buffered(1))

    return pl.pallas_call(
        functools.partial(_proj_kernel, tm=tm, tn=tn),
        grid=(t // tm, n // tn),
        in_specs=[pl.BlockSpec((tm, d), lambda i, j: (i, 0)),
                  pl.BlockSpec((None, d, tn), lambda i, j: (layer, 0, j)),
                  pl.BlockSpec((1, tn), lambda i, j: (0, j)),
                  per_row_tile(half), per_row_tile(half), _resident(perms)],
        out_specs=pl.BlockSpec((tm, tn), lambda i, j: (i, j)),
        out_shape=jax.ShapeDtypeStruct((t, n), BF16),
        scratch_shapes=[pltpu.VMEM((len(ATTN_DILATIONS), tm, d), BF16)],
        compiler_params=_params(("arbitrary", "arbitrary")),
        name="in_proj",
    )(x2d, w_in, col_scale, cos, sin, perms)


def _attn_blocks(refs, o_ref, o_scr, lse_scr, out_scr, has_prev):
    (q0, k0, v0, q1, k1, v1, q2, k2, v2, kp0, vp0) = refs[:11]
    n1 = ATTN_DILATIONS[1]
    kp1 = refs[11:11 + n1]
    vp1 = refs[11 + n1:11 + 2 * n1]
    kp2, vp2 = refs[11 + 2 * n1:]
    blk = ATTN_BLK

    def rows(start):
        return slice(start, start + blk)

    row = lax.broadcasted_iota(jnp.int32, (blk, 2 * blk), 0)
    col = lax.broadcasted_iota(jnp.int32, (blk, 2 * blk), 1)
    dist = row + blk - col
    band = jnp.logical_and(dist >= 0, dist <= blk)
    bias_mid = jnp.where(band, 0.0, -jnp.inf)
    bias_first = jnp.where(jnp.logical_and(band, jnp.logical_or(col >= blk, has_prev)), 0.0, -jnp.inf)
    ones = jnp.ones((2 * blk, HEAD_DIM), BF16)

    blocks = []
    for n in range(ATTN_TILE // blk):
        blocks.append((q0, k0, v0, n * blk, (kp0, vp0, 0) if n == 0 else None, 0))
    for gi, (q, k, v) in ((1, (q1, k1, v1)), (2, (q2, k2, v2))):
        dil = ATTN_DILATIONS[gi]
        per = ATTN_TILE // dil
        for r in range(dil):
            for n in range(per // blk):
                base = r * per + n * blk
                prev = None
                if n == 0:
                    prev = (kp1[r], vp1[r], 0) if gi == 1 else (kp2, vp2, base)
                blocks.append((q, k, v, base, prev, gi))

    def prev_and_cur(b, which):
        base, prev = b[3], b[4]
        if prev is None:
            return b[which][base - blk:base + blk, :]
        return jnp.concatenate([prev[which - 1][rows(prev[2]), :], b[which][rows(base), :]], axis=0)

    def scores(b):
        return _dot_nt(b[0][rows(b[3]), :], prev_and_cur(b, 1))

    def softmax_pv(s, b):
        m = jnp.max(s, axis=-1, keepdims=True)
        p = jnp.exp2(s - m)
        pv = _dot(p.astype(BF16), jnp.concatenate([prev_and_cur(b, 2), ones], axis=1))
        l = pv[:, HEAD_DIM:]
        return pv[:, :HEAD_DIM] * (1.0 / l), m + jnp.log2(l)

    n_stored = len(blocks) - ATTN_DILATIONS[2]
    ahead = 1
    s_queue = [scores(blocks[i]) for i in range(ahead)]
    for i, b in enumerate(blocks):
        s = s_queue.pop(0) + (bias_mid if b[4] is None else bias_first)
        if i + ahead < len(blocks):
            s_queue.append(scores(blocks[i + ahead]))
        o2, lse2 = softmax_pv(s, b)
        if i < n_stored:
            gi = b[5]
            o_scr[gi, rows(b[3]), :] = o2
            lse_scr[gi, rows(b[3]), :] = lse2
            yield
            continue
        r = i - n_stored
        d1, d2 = ATTN_DILATIONS[1], ATTN_DILATIONS[2]
        src0 = pl.ds(r, blk, stride=d2)
        src1 = pl.ds((r % d1) * (ATTN_TILE // d1) + r // d1, blk, stride=d2 // d1)
        lse0, lse1 = lse_scr[0, src0, :], lse_scr[1, src1, :]
        mx = jnp.maximum(jnp.maximum(lse0, lse1), lse2)
        e0, e1, e2 = jnp.exp2(lse0 - mx), jnp.exp2(lse1 - mx), jnp.exp2(lse2 - mx)
        num = e0 * o_scr[0, src0, :] + e1 * o_scr[1, src1, :] + e2 * o2
        out_scr[src0, :] = num * (1.0 / (e0 + e1 + e2))
        yield
    o_ref[...] = out_scr[...].astype(o_ref.dtype)


def _ret_chunks(q_ref, k_ref, v_ref, intra_ref, qd_ref, kd_ref, cd_ref, o_ref, state_ref, rows):
    n_chunks = rows // RET_CHUNK

    def chunk(c):
        return slice(c * RET_CHUNK, (c + 1) * RET_CHUNK)

    def independent(c):
        q, k = q_ref[0, chunk(c), :], k_ref[0, chunk(c), :]
        kd = (k.astype(F32) * kd_ref[0]).astype(BF16)
        return _dot_nt(q, k), _dot_tn(kd, v_ref[0, chunk(c), :])

    ahead = 2
    queue = [independent(c) for c in range(ahead)]
    state = state_ref[...]
    for c in range(n_chunks):
        scores, update = queue.pop(0)
        qd = (q_ref[0, chunk(c), :].astype(F32) * qd_ref[0]).astype(BF16)
        y = _dot((scores * intra_ref[0]).astype(BF16), v_ref[0, chunk(c), :]) + _dot(qd, state.astype(BF16))
        if c + ahead < n_chunks:
            queue.append(independent(c + ahead))
        state = cd_ref[0] * state + update
        o_ref[0, chunk(c), :] = y.astype(o_ref.dtype)
        if c + 1 == n_chunks:
            state_ref[...] = state
        yield


def _mixer_kernel(*refs, n_attn_in, rows):
    attn_in = refs[:n_attn_in]
    ret_in = refs[n_attn_in:n_attn_in + 7]
    attn_out, ret_out, o_scr, lse_scr, out_scr, state_ref = refs[n_attn_in + 7:]
    n = pl.program_id(2)

    @pl.when(n == 0)
    def _():
        state_ref[...] = jnp.zeros_like(state_ref)

    steps_per_tile = ATTN_TILE // rows
    attn = _attn_blocks(attn_in, attn_out, o_scr, lse_scr, out_scr, n // steps_per_tile > 0)
    ret = _ret_chunks(*ret_in, ret_out, state_ref, rows)
    for _ in ret:
        pass
    for _ in attn:
        pass


def _token_mixers(proj2d, decays, batch, seq, rows=1024):
    t = proj2d.shape[0]
    intra, qd, kd, cd = decays
    tiles_per_seq = seq // ATTN_TILE
    steps_per_tile = ATTN_TILE // rows
    heads_per_step = ATTN_HEADS // RET_HEADS
    assert steps_per_tile == heads_per_step and seq % ATTN_TILE == 0
    heads_per_group = 3 * ATTN_HEADS
    n1, n2 = ATTN_DILATIONS[1], ATTN_DILATIONS[2]
    blocks_per_tile = ATTN_TILE // ATTN_BLK

    def tile(bi, n):
        return bi * tiles_per_seq + n // steps_per_tile

    def prev_tile(bi, n):
        return bi * tiles_per_seq + jnp.maximum(n // steps_per_tile - 1, 0)

    def head(hr, n):
        return (n % steps_per_tile) * RET_HEADS + hr

    def col(gi, which, hr, n):
        return gi * heads_per_group + which * ATTN_HEADS + head(hr, n)

    def cur(gi, which):
        return pl.BlockSpec((ATTN_TILE, HEAD_DIM), lambda bi, hr, n: (tile(bi, n), col(gi, which, hr, n)))

    def prev_blk(gi, which, blk_in_tile):
        return pl.BlockSpec((ATTN_BLK, HEAD_DIM),
                            lambda bi, hr, n: (prev_tile(bi, n) * blocks_per_tile + blk_in_tile,
                                               col(gi, which, hr, n)))

    def prev_whole(gi, which):
        return pl.BlockSpec((ATTN_TILE, HEAD_DIM), lambda bi, hr, n: (prev_tile(bi, n), col(gi, which, hr, n)))

    per1 = ATTN_TILE // n1 // ATTN_BLK
    attn_specs = [cur(g, w) for g in range(3) for w in range(3)]
    attn_specs += [prev_blk(0, 1, blocks_per_tile - 1), prev_blk(0, 2, blocks_per_tile - 1)]
    attn_specs += [prev_blk(1, 1, r * per1 + per1 - 1) for r in range(n1)]
    attn_specs += [prev_blk(1, 2, r * per1 + per1 - 1) for r in range(n1)]
    attn_specs += [prev_whole(2, 1), prev_whole(2, 2)]
    assert ATTN_TILE // n2 == ATTN_BLK

    steps_per_seq = seq // rows
    qb, kb, vb = COL_RET_Q // RET_KEY_DIM, COL_RET_K // RET_KEY_DIM, COL_RET_V // RET_VAL_DIM
    c = RET_CHUNK

    def ret_rows(width, cb):
        return pl.BlockSpec((1, rows, width), lambda bi, hr, n: (bi * steps_per_seq + n, 0, cb + hr))

    def per_head(shape):
        return pl.BlockSpec((1,) + shape, lambda bi, hr, n: (hr, 0, 0))

    proj3d = proj2d.reshape(t // rows, rows, N_IN)
    ret_specs = [ret_rows(RET_KEY_DIM, qb), ret_rows(RET_KEY_DIM, kb), ret_rows(RET_VAL_DIM, vb),
                 per_head((c, c)), per_head((c, RET_KEY_DIM)), per_head((c, RET_KEY_DIM)),
                 per_head((1, RET_VAL_DIM))]
    attn, ret = pl.pallas_call(
        functools.partial(_mixer_kernel, n_attn_in=len(attn_specs), rows=rows),
        grid=(batch, RET_HEADS, steps_per_seq),
        in_specs=attn_specs + ret_specs,
        out_specs=[pl.BlockSpec((ATTN_TILE, HEAD_DIM), lambda bi, hr, n: (tile(bi, n), head(hr, n))),
                   ret_rows(RET_VAL_DIM, 0)],
        out_shape=[jax.ShapeDtypeStruct((t, ATTN_WIDTH), BF16),
                   jax.ShapeDtypeStruct((t // rows, rows, RET_HEADS * RET_VAL_DIM), BF16)],
        scratch_shapes=[pltpu.VMEM((2, ATTN_TILE, HEAD_DIM), F32)] * 2
                       + [pltpu.VMEM((ATTN_TILE, HEAD_DIM), F32), pltpu.VMEM((RET_KEY_DIM, RET_VAL_DIM), F32)],
        compiler_params=_params(("arbitrary", "arbitrary", "arbitrary")),
        name="token_mixers",
    )(*([proj2d] * len(attn_specs)), proj3d, proj3d, proj3d, intra, qd, kd, cd)
    return attn, ret.reshape(t, RET_HEADS * RET_VAL_DIM)


def _retention_decays():
    c = RET_CHUNK
    log_gamma = jnp.log1p(-jnp.exp2(-5.0 - jnp.arange(RET_HEADS, dtype=F32)))
    idx = jnp.arange(c, dtype=F32)
    rel = idx[:, None] - idx[None, :]
    intra = jnp.where(rel >= 0, jnp.exp(log_gamma[:, None, None] * jnp.maximum(rel, 0.0)), 0.0)
    qd = jnp.exp(log_gamma[:, None] * (idx + 1.0))[:, :, None]
    kd = jnp.exp(log_gamma[:, None] * (c - 1.0 - idx))[:, :, None]
    cd = jnp.exp(log_gamma * c)[:, None, None]
    qd = jnp.broadcast_to(qd, (RET_HEADS, c, RET_KEY_DIM))
    kd = jnp.broadcast_to(kd, (RET_HEADS, c, RET_KEY_DIM))
    cd = jnp.broadcast_to(cd, (RET_HEADS, 1, RET_VAL_DIM))
    return intra, qd, kd, cd


def _merge_kernel(attn_ref, ret_ref, rg0_ref, rg1_ref, ga_ref, gr_ref, x_ref, p_ref, wa_ref, wr_ref, wo_ref,
                  wpg_ref, wpp_ref, gng_ref, gnb_ref, g1_ref, b1_ref, x1b_ref, resid_ref, *, sub):
    blocks = [slice(r, r + sub) for r in range(0, attn_ref.shape[0], sub)]

    def retention_proj(rs):
        pr = None
        for h in range(RET_HEADS):
            cs = slice(h * RET_VAL_DIM, (h + 1) * RET_VAL_DIM)
            y = ret_ref[rs, cs].astype(F32)
            mu = jnp.mean(y, axis=-1, keepdims=True)
            d = y - mu
            var = jnp.mean(d * d, axis=-1, keepdims=True)
            yn = d * lax.rsqrt(var + GN_EPS) * gng_ref[:, cs] + gnb_ref[:, cs]
            half = RET_HEADS // 2
            g_ref = rg0_ref if h < half else rg1_ref
            g = g_ref[rs, (h % half) * RET_VAL_DIM:(h % half + 1) * RET_VAL_DIM].astype(F32)
            part = _dot(((g * _sigmoid(g)) * yn).astype(BF16), wr_ref[cs, :])
            pr = part if pr is None else pr + part
        return pr

    first, ret = [], []
    for rs in blocks:
        pa = _dot(attn_ref[rs, :], wa_ref[...])
        ret.append(retention_proj(rs))
        first.append((pa, _dot(p_ref[rs, :].astype(BF16), wpp_ref[...])))
    mix = []
    for rs, (pa, _), pr in zip(blocks, first, ret):
        merged = _sigmoid(ga_ref[rs, :].astype(F32)) * pa + _sigmoid(gr_ref[rs, :].astype(F32)) * pr
        mix.append(_dot(merged.astype(BF16), wo_ref[...]))
    x1s, gates = [], []
    for rs, m in zip(blocks, mix):
        x1 = _layer_norm(DN_ALPHA * x_ref[rs, :] + m, g1_ref[...], b1_ref[...])
        x1b = x1.astype(BF16)
        x1b_ref[rs, :] = x1b
        x1s.append(x1)
        gates.append(_dot(x1b, wpg_ref[...]))
    for rs, x1, gate, (_, pp) in zip(blocks, x1s, gates, first):
        resid_ref[rs, :] = DN_ALPHA * x1 + _sigmoid(gate) * pp


def _merge(attn, ret, proj2d, x2d, p3d, wa, wr, wo, wpg, wpp, gn_g, gn_b, g1, b1, layer, tm=512, sub=256):
    t = x2d.shape[0]

    def rows(width, cb=0):
        return pl.BlockSpec((tm, width), lambda i: (i, cb))

    ret_width = RET_HEADS * RET_VAL_DIM
    gate_width = ret_width // 2
    assert COL_RET_G % gate_width == 0
    return pl.pallas_call(
        functools.partial(_merge_kernel, sub=sub),
        grid=(t // tm,),
        in_specs=[rows(ATTN_WIDTH), rows(ret_width),
                  rows(gate_width, COL_RET_G // gate_width), rows(gate_width, COL_RET_G // gate_width + 1),
                  rows(D_MODEL, COL_GATE_A // D_MODEL), rows(D_MODEL, COL_GATE_R // D_MODEL),
                  rows(D_MODEL), pl.BlockSpec((None, tm, PLE_DIM), lambda i: (layer, i, 0)),
                  _resident(wa, layer), _resident(wr, layer), _resident(wo, layer), _resident(wpg, layer),
                  _resident(wpp, layer), _resident(gn_g, layer), _resident(gn_b, layer),
                  _resident(g1, layer), _resident(b1, layer)],
        out_specs=[rows(D_MODEL), rows(D_MODEL)],
        out_shape=[jax.ShapeDtypeStruct((t, D_MODEL), BF16), jax.ShapeDtypeStruct((t, D_MODEL), F32)],
        compiler_params=_params(("arbitrary",)),
        name="merge_ln1_ple",
    )(attn, ret, proj2d, proj2d, proj2d, proj2d, x2d, p3d, wa, wr, wo, wpg, wpp, gn_g, gn_b, g1, b1)


FFN_HALO = 16


def _ffn_kernel(x_ref, halo_ref, wup_ref, cw_ref, cb_ref, wd_ref, resid_ref, g2_ref, b2_ref, o_ref,
                xcat_ref, *, tm, sub, tf, tiles_per_seq):
    first = (pl.program_id(0) % tiles_per_seq) == 0
    halo = halo_ref[...]
    xcat_ref[0:FFN_HALO, :] = jnp.where(first, jnp.zeros_like(halo), halo)
    xcat_ref[FFN_HALO:, :] = x_ref[...]

    c0 = 0.7978845608028654
    pairs = [(r, c) for r in range(tm // sub) for c in range(D_FF // tf)]

    def up_proj(k):
        r, c = pairs[k]
        xc = xcat_ref[r * sub:r * sub + sub + FFN_HALO, :]
        return (_dot(xc, wup_ref[:, c * tf:(c + 1) * tf]),
                _dot(xc, wup_ref[:, D_FF + c * tf:D_FF + (c + 1) * tf]))

    def conv(h, col0):
        cs = slice(col0, col0 + tf)
        return (cw_ref[2:3, cs] * h[FFN_HALO:FFN_HALO + sub, :]
                + cw_ref[1:2, cs] * h[FFN_HALO - 1:FFN_HALO - 1 + sub, :]
                + cw_ref[0:1, cs] * h[FFN_HALO - 2:FFN_HALO - 2 + sub, :]
                + cb_ref[:, cs])

    ahead = 2
    h_queue = [up_proj(k) for k in range(ahead)]
    acc = None
    for k, (r, c) in enumerate(pairs):
        hg, hu = h_queue.pop(0)
        if k + ahead < len(pairs):
            h_queue.append(up_proj(k + ahead))
        gate = conv(hg, c * tf)
        up = conv(hu, D_FF + c * tf)
        half_gate = 0.5 * gate
        gelu = half_gate + half_gate * jnp.tanh(gate * (c0 + (c0 * 0.044715) * (gate * gate)))
        d = _dot((gelu * up).astype(BF16), wd_ref[c * tf:(c + 1) * tf, :])
        acc = d if c == 0 else acc + d
        if c == D_FF // tf - 1:
            rs = slice(r * sub, (r + 1) * sub)
            o_ref[rs, :] = _layer_norm(resid_ref[rs, :] + acc, g2_ref[...], b2_ref[...])


def _ffn(x1b, resid, w_up, conv_w, conv_b, w_down, g2, b2, layer, seq, tm=512, tf=256, sub=256):
    t = x1b.shape[0]
    hb = tm // FFN_HALO

    return pl.pallas_call(
        functools.partial(_ffn_kernel, tm=tm, sub=sub, tf=tf, tiles_per_seq=seq // tm),
        grid=(t // tm,),
        in_specs=[pl.BlockSpec((tm, D_MODEL), lambda i: (i, 0)),
                  pl.BlockSpec((FFN_HALO, D_MODEL), lambda i: (jnp.maximum(i * hb - 1, 0), 0)),
                  _resident(w_up, layer), _resident(conv_w, layer), _resident(conv_b, layer),
                  _resident(w_down, layer),
                  pl.BlockSpec((tm, D_MODEL), lambda i: (i, 0)),
                  _resident(g2, layer), _resident(b2, layer)],
        out_specs=pl.BlockSpec((tm, D_MODEL), lambda i: (i, 0)),
        out_shape=jax.ShapeDtypeStruct((t, D_MODEL), F32),
        scratch_shapes=[pltpu.VMEM((tm + FFN_HALO, D_MODEL), BF16)],
        compiler_params=_params(("arbitrary",)),
        name="conv_ffn_ln2",
    )(x1b, x1b, w_up, conv_w, conv_b, w_down, resid, g2, b2)


def kernel(x, p, positions, w_in, w_attn_proj, w_ret_proj, ret_gn_g, ret_gn_b, w_out, ln1_g, ln1_b, w_up,
           conv_w, conv_b, w_down, w_ple_gate, w_ple_proj, ln2_g, ln2_b):
    b, s, d = x.shape
    t = b * s
    assert s % ATTN_TILE == 0
    cos, sin = _rope_tables(positions)
    decays = _retention_decays()
    col_scale = _proj_col_scale()
    perms = _row_permutations()
    x2d = x.reshape(t, d)
    depth = w_in.shape[0]
    p3d = p.reshape(depth, t, PLE_DIM)
    row = lambda v: v.reshape(depth, 1, -1)
    wa, wr, wo = w_attn_proj.astype(BF16), w_ret_proj.astype(BF16), w_out.astype(BF16)
    wpg, wpp = w_ple_gate.astype(BF16), w_ple_proj.astype(BF16)
    wup, wdn = w_up.astype(BF16), w_down.astype(BF16)
    for i in range(depth):
        proj2d = _in_proj(x2d, w_in, i, col_scale, cos, sin, perms)
        attn, ret = _token_mixers(proj2d, decays, b, s)
        x1b, resid = _merge(attn, ret, proj2d, x2d, p3d, wa, wr, wo, wpg, wpp, row(ret_gn_g), row(ret_gn_b),
                            row(ln1_g), row(ln1_b), i)
        x2d = _ffn(x1b, resid, wup, conv_w, row(conv_b), wdn, row(ln2_g), row(ln2_b), i, s)
    return x2d.reshape(b, s, d)
```

```python
import functools

import numpy as np
import jax
import jax.numpy as jnp
from jax import lax
from jax.experimental import pallas as pl
from jax.experimental.pallas import tpu as pltpu

F32 = jnp.float32
BF16 = jnp.bfloat16

D_MODEL = 1024
ATTN_DILATIONS = (1, 4, 16)
ATTN_HEADS = 8
HEAD_DIM = 128
ATTN_WIDTH = ATTN_HEADS * HEAD_DIM
RET_HEADS = 4
RET_KEY_DIM = 256
RET_VAL_DIM = 512
RET_CHUNK = 128
ROPE_BASE = 10000.0
D_FF = 2816
PLE_DIM = 256
N_IN = 17408
DEPTH = 2
DN_ALPHA = (2 * DEPTH) ** 0.25
LN_EPS = 1e-5
GN_EPS = 1e-6
LOG2E = 1.4426950408889634

COL_RET_Q = 3 * 3 * ATTN_WIDTH
COL_RET_K = COL_RET_Q + RET_HEADS * RET_KEY_DIM
COL_RET_V = COL_RET_K + RET_HEADS * RET_KEY_DIM
COL_RET_G = COL_RET_V + RET_HEADS * RET_VAL_DIM
COL_GATE_A = COL_RET_G + RET_HEADS * RET_VAL_DIM
COL_GATE_R = COL_GATE_A + D_MODEL

ATTN_BLK = 128
ATTN_TILE = ATTN_BLK * max(ATTN_DILATIONS)
VMEM_LIMIT = 56 * 1024 * 1024


def _params(sem):
    return pltpu.CompilerParams(dimension_semantics=sem, vmem_limit_bytes=VMEM_LIMIT)


def _resident(arr, layer=None):
    if layer is None:
        return pl.BlockSpec(arr.shape, lambda *_: (0,) * arr.ndim, pipeline_mode=pl.Buffered(1))
    return pl.BlockSpec((None,) + arr.shape[1:], lambda *_: (layer,) + (0,) * (arr.ndim - 1),
                        pipeline_mode=pl.Buffered(1))


def _dot(a, b):
    return jnp.dot(a, b, preferred_element_type=F32)


def _dot_nt(a, b):
    return lax.dot_general(a, b, (((1,), (1,)), ((), ())), preferred_element_type=F32)


def _dot_tn(a, b):
    return lax.dot_general(a, b, (((0,), (0,)), ((), ())), preferred_element_type=F32)


def _sigmoid(x):
    return 1.0 / (1.0 + jnp.exp2(x * -LOG2E))


def _layer_norm(y, g, b):
    mu = jnp.mean(y, axis=-1, keepdims=True)
    d = y - mu
    var = jnp.mean(d * d, axis=-1, keepdims=True)
    return d * lax.rsqrt(var + LN_EPS) * g + b


def _rope_kernel(pos_ref, freq_ref, cos_ref, sin_ref):
    ang = pos_ref[...] * freq_ref[...]
    cos_ref[...] = jnp.cos(ang)
    sin_ref[...] = jnp.sin(ang)


def _rope_tables(positions):
    b, s = positions.shape
    half = RET_KEY_DIM // 2
    pos = jnp.broadcast_to(positions.astype(F32).reshape(b * s, 1), (b * s, half))
    freq = jnp.power(ROPE_BASE, -jnp.arange(half, dtype=F32) / half).reshape(1, half)
    rows = 2048
    spec = pl.BlockSpec((rows, half), lambda i: (i, 0))
    cos, sin = pl.pallas_call(
        _rope_kernel,
        grid=(b * s // rows,),
        in_specs=[spec, pl.BlockSpec((1, half), lambda i: (0, 0))],
        out_specs=[spec, spec],
        out_shape=[jax.ShapeDtypeStruct((b * s, half), F32)] * 2,
        compiler_params=_params(("arbitrary",)),
        name="rope_tables",
    )(pos, freq)
    return cos, sin


PERM_BLK = 256


def _proj_kernel(x_ref, w_ref, cs_ref, cos_ref, sin_ref, perm_ref, o_ref, xb_ref, *, tm, tn):
    j = pl.program_id(1)

    @pl.when(j == 0)
    def _():
        xb_ref[0] = x_ref[...].astype(BF16)
        for blk in range(tm // PERM_BLK):
            xblk = xb_ref[0, blk * PERM_BLK:(blk + 1) * PERM_BLK, :]
            for gi, dil in enumerate(ATTN_DILATIONS):
                if dil == 1:
                    continue
                y = _dot(perm_ref[gi - 1], xblk).astype(BF16)
                n = PERM_BLK // dil
                for r in range(dil):
                    dst = r * (tm // dil) + blk * n
                    xb_ref[gi, dst:dst + n, :] = y[r * n:(r + 1) * n, :]

    grp = (j * tn) // (3 * ATTN_WIDTH)
    sel = jnp.where(grp < len(ATTN_DILATIONS), grp, 0)
    is_rope = jnp.logical_and(j >= COL_RET_Q // tn, j < COL_RET_V // tn)

    @pl.when(jnp.logical_not(is_rope))
    def _():
        o_ref[...] = (_dot(xb_ref[sel], w_ref[...].astype(BF16)) * cs_ref[...]).astype(o_ref.dtype)

    @pl.when(is_rope)
    def _():
        y = _dot(xb_ref[0], w_ref[...].astype(BF16)) * cs_ref[...]
        cos, sin = cos_ref[...], sin_ref[...]
        half = RET_KEY_DIM // 2
        for h0 in range(0, tn, RET_KEY_DIM):
            t1, t2 = y[:, h0:h0 + half], y[:, h0 + half:h0 + RET_KEY_DIM]
            o_ref[:, h0:h0 + half] = (t1 * cos - t2 * sin).astype(o_ref.dtype)
            o_ref[:, h0 + half:h0 + RET_KEY_DIM] = (t1 * sin + t2 * cos).astype(o_ref.dtype)


def _proj_col_scale():
    scale = np.ones((1, N_IN), np.float32)
    for g in range(len(ATTN_DILATIONS)):
        scale[:, g * 3 * ATTN_WIDTH:g * 3 * ATTN_WIDTH + ATTN_WIDTH] = (HEAD_DIM ** -0.5) * LOG2E
    scale[:, COL_RET_K:COL_RET_V] = RET_KEY_DIM ** -0.5
    return jnp.asarray(scale)


def _row_permutations():
    mats = []
    for dil in ATTN_DILATIONS[1:]:
        m = np.zeros((PERM_BLK, PERM_BLK), np.float32)
        u = np.arange(PERM_BLK)
        m[(u % dil) * (PERM_BLK // dil) + u // dil, u] = 1.0
        mats.append(m)
    return jnp.asarray(np.stack(mats), dtype=BF16)


def _in_proj(x2d, w_in, layer, col_scale, cos, sin, perms, tm=ATTN_TILE, tn=1024):
    t, d = x2d.shape
    n = w_in.shape[2]
    half = RET_KEY_DIM // 2
    assert COL_RET_Q % tn == 0 and COL_RET_V % tn == 0 and tn % RET_KEY_DIM == 0

    def per_row_tile(width):
        return pl.BlockSpec((tm, width), lambda i, j: (i, 0))

    return pl.pallas_call(
        functools.partial(_proj_kernel, tm=tm, tn=tn),
        grid=(t // tm, n // tn),
        in_specs=[per_row_tile(d),
                  pl.BlockSpec((None, d, tn), lambda i, j: (layer, 0, j)),
                  pl.BlockSpec((1, tn), lambda i, j: (0, j)),
                  per_row_tile(half), per_row_tile(half), _resident(perms)],
        out_specs=pl.BlockSpec((tm, tn), lambda i, j: (i, j)),
        out_shape=jax.ShapeDtypeStruct((t, n), BF16),
        scratch_shapes=[pltpu.VMEM((len(ATTN_DILATIONS), tm, d), BF16)],
        compiler_params=_params(("arbitrary", "arbitrary")),
        name="in_proj",
    )(x2d, w_in, col_scale, cos, sin, perms)


def _attn_blocks(refs, o_ref, o_scr, lse_scr, out_scr, has_prev):
    (q0, k0, v0, q1, k1, v1, q2, k2, v2, kp0, vp0) = refs[:11]
    n1 = ATTN_DILATIONS[1]
    kp1 = refs[11:11 + n1]
    vp1 = refs[11 + n1:11 + 2 * n1]
    kp2, vp2 = refs[11 + 2 * n1:]
    blk = ATTN_BLK

    def rows(start):
        return slice(start, start + blk)

    row = lax.broadcasted_iota(jnp.int32, (blk, 2 * blk), 0)
    col = lax.broadcasted_iota(jnp.int32, (blk, 2 * blk), 1)
    dist = row + blk - col
    band = jnp.logical_and(dist >= 0, dist <= blk)
    bias_mid = jnp.where(band, 0.0, -jnp.inf)
    bias_first = jnp.where(jnp.logical_and(band, jnp.logical_or(col >= blk, has_prev)), 0.0, -jnp.inf)
    ones = jnp.ones((2 * blk, HEAD_DIM), BF16)

    blocks = []
    for n in range(ATTN_TILE // blk):
        blocks.append((q0, k0, v0, n * blk, (kp0, vp0, 0) if n == 0 else None, 0))
    for gi, (q, k, v) in ((1, (q1, k1, v1)), (2, (q2, k2, v2))):
        dil = ATTN_DILATIONS[gi]
        per = ATTN_TILE // dil
        for r in range(dil):
            for n in range(per // blk):
                base = r * per + n * blk
                prev = None
                if n == 0:
                    prev = (kp1[r], vp1[r], 0) if gi == 1 else (kp2, vp2, base)
                blocks.append((q, k, v, base, prev, gi))

    def prev_and_cur(b, which):
        base, prev = b[3], b[4]
        if prev is None:
            return b[which][base - blk:base + blk, :]
        return jnp.concatenate([prev[which - 1][rows(prev[2]), :], b[which][rows(base), :]], axis=0)

    def scores(b):
        return _dot_nt(b[0][rows(b[3]), :], prev_and_cur(b, 1))

    def softmax_pv(s, b):
        m = jnp.max(s, axis=-1, keepdims=True)
        p = jnp.exp2(s - m)
        pv = _dot(p.astype(BF16), jnp.concatenate([prev_and_cur(b, 2), ones], axis=1))
        l = pv[:, HEAD_DIM:]
        return pv[:, :HEAD_DIM] * (1.0 / l), m + jnp.log2(l)

    n_stored = len(blocks) - ATTN_DILATIONS[2]
    ahead = 1
    s_queue = [scores(blocks[i]) for i in range(ahead)]
    for i, b in enumerate(blocks):
        s = s_queue.pop(0) + (bias_mid if b[4] is None else bias_first)
        if i + ahead < len(blocks):
            s_queue.append(scores(blocks[i + ahead]))
        o2, lse2 = softmax_pv(s, b)
        if i < n_stored:
            gi = b[5]
            o_scr[gi, rows(b[3]), :] = o2
            lse_scr[gi, rows(b[3]), :] = lse2
            yield
            continue
        r = i - n_stored
        d1, d2 = ATTN_DILATIONS[1], ATTN_DILATIONS[2]
        src0 = pl.ds(r, blk, stride=d2)
        src1 = pl.ds((r % d1) * (ATTN_TILE // d1) + r // d1, blk, stride=d2 // d1)
        lse0, lse1 = lse_scr[0, src0, :], lse_scr[1, src1, :]
        mx = jnp.maximum(jnp.maximum(lse0, lse1), lse2)
        e0, e1, e2 = jnp.exp2(lse0 - mx), jnp.exp2(lse1 - mx), jnp.exp2(lse2 - mx)
        num = e0 * o_scr[0, src0, :] + e1 * o_scr[1, src1, :] + e2 * o2
        out_scr[src0, :] = num * (1.0 / (e0 + e1 + e2))
        yield
    o_ref[...] = out_scr[...].astype(o_ref.dtype)


def _ret_chunks(q_ref, k_ref, v_ref, intra_ref, qd_ref, kd_ref, cd_ref, o_ref, state_ref, rows):
    n_chunks = rows // RET_CHUNK

    def chunk(c):
        return slice(c * RET_CHUNK, (c + 1) * RET_CHUNK)

    def independent(c):
        q, k = q_ref[0, chunk(c), :], k_ref[0, chunk(c), :]
        kd = (k.astype(F32) * kd_ref[0]).astype(BF16)
        return _dot_nt(q, k), _dot_tn(kd, v_ref[0, chunk(c), :])

    ahead = 2
    queue = [independent(c) for c in range(ahead)]
    state = state_ref[...]
    for c in range(n_chunks):
        scores, update = queue.pop(0)
        qd = (q_ref[0, chunk(c), :].astype(F32) * qd_ref[0]).astype(BF16)
        y = _dot((scores * intra_ref[0]).astype(BF16), v_ref[0, chunk(c), :]) + _dot(qd, state.astype(BF16))
        if c + ahead < n_chunks:
            queue.append(independent(c + ahead))
        state = cd_ref[0] * state + update
        o_ref[0, chunk(c), :] = y.astype(o_ref.dtype)
        if c + 1 == n_chunks:
            state_ref[...] = state
        yield


def _mixer_kernel(*refs, n_attn_in, rows):
    attn_in = refs[:n_attn_in]
    ret_in = refs[n_attn_in:n_attn_in + 7]
    attn_out, ret_out, o_scr, lse_scr, out_scr, state_ref = refs[n_attn_in + 7:]
    n = pl.program_id(2)

    @pl.when(n == 0)
    def _():
        state_ref[...] = jnp.zeros_like(state_ref)

    steps_per_tile = ATTN_TILE // rows
    attn = _attn_blocks(attn_in, attn_out, o_scr, lse_scr, out_scr, n // steps_per_tile > 0)
    ret = _ret_chunks(*ret_in, ret_out, state_ref, rows)
    for _ in ret:
        pass
    for _ in attn:
        pass


def _token_mixers(proj2d, decays, batch, seq, rows=1024):
    t = proj2d.shape[0]
    intra, qd, kd, cd = decays
    tiles_per_seq = seq // ATTN_TILE
    steps_per_tile = ATTN_TILE // rows
    heads_per_step = ATTN_HEADS // RET_HEADS
    assert steps_per_tile == heads_per_step and seq % ATTN_TILE == 0
    heads_per_group = 3 * ATTN_HEADS
    n1, n2 = ATTN_DILATIONS[1], ATTN_DILATIONS[2]
    blocks_per_tile = ATTN_TILE // ATTN_BLK

    def tile(bi, n):
        return bi * tiles_per_seq + n // steps_per_tile

    def prev_tile(bi, n):
        return bi * tiles_per_seq + jnp.maximum(n // steps_per_tile - 1, 0)

    def head(hr, n):
        return (n % steps_per_tile) * RET_HEADS + hr

    def col(gi, which, hr, n):
        return gi * heads_per_group + which * ATTN_HEADS + head(hr, n)

    def cur(gi, which):
        return pl.BlockSpec((ATTN_TILE, HEAD_DIM), lambda bi, hr, n: (tile(bi, n), col(gi, which, hr, n)))

    def prev_blk(gi, which, blk_in_tile):
        return pl.BlockSpec((ATTN_BLK, HEAD_DIM),
                            lambda bi, hr, n: (prev_tile(bi, n) * blocks_per_tile + blk_in_tile,
                                               col(gi, which, hr, n)))

    def prev_whole(gi, which):
        return pl.BlockSpec((ATTN_TILE, HEAD_DIM), lambda bi, hr, n: (prev_tile(bi, n), col(gi, which, hr, n)))

    per1 = ATTN_TILE // n1 // ATTN_BLK
    attn_specs = [cur(g, w) for g in range(3) for w in range(3)]
    attn_specs += [prev_blk(0, 1, blocks_per_tile - 1), prev_blk(0, 2, blocks_per_tile - 1)]
    attn_specs += [prev_blk(1, 1, r * per1 + per1 - 1) for r in range(n1)]
    attn_specs += [prev_blk(1, 2, r * per1 + per1 - 1) for r in range(n1)]
    attn_specs += [prev_whole(2, 1), prev_whole(2, 2)]
    assert ATTN_TILE // n2 == ATTN_BLK

    steps_per_seq = seq // rows
    qb, kb, vb = COL_RET_Q // RET_KEY_DIM, COL_RET_K // RET_KEY_DIM, COL_RET_V // RET_VAL_DIM
    c = RET_CHUNK

    def ret_rows(width, cb):
        return pl.BlockSpec((1, rows, width), lambda bi, hr, n: (bi * steps_per_seq + n, 0, cb + hr))

    def per_head(shape):
        return pl.BlockSpec((1,) + shape, lambda bi, hr, n: (hr, 0, 0))

    proj3d = proj2d.reshape(t // rows, rows, N_IN)
    ret_specs = [ret_rows(RET_KEY_DIM, qb), ret_rows(RET_KEY_DIM, kb), ret_rows(RET_VAL_DIM, vb),
                 per_head((c, c)), per_head((c, RET_KEY_DIM)), per_head((c, RET_KEY_DIM)),
                 per_head((1, RET_VAL_DIM))]
    attn, ret = pl.pallas_call(
        functools.partial(_mixer_kernel, n_attn_in=len(attn_specs), rows=rows),
        grid=(batch, RET_HEADS, steps_per_seq),
        in_specs=attn_specs + ret_specs,
        out_specs=[pl.BlockSpec((ATTN_TILE, HEAD_DIM), lambda bi, hr, n: (tile(bi, n), head(hr, n))),
                   ret_rows(RET_VAL_DIM, 0)],
        out_shape=[jax.ShapeDtypeStruct((t, ATTN_WIDTH), BF16),
                   jax.ShapeDtypeStruct((t // rows, rows, RET_HEADS * RET_VAL_DIM), BF16)],
        scratch_shapes=[pltpu.VMEM((2, ATTN_TILE, HEAD_DIM), F32)] * 2
                       + [pltpu.VMEM((ATTN_TILE, HEAD_DIM), F32), pltpu.VMEM((RET_KEY_DIM, RET_VAL_DIM), F32)],
        compiler_params=_params(("arbitrary", "arbitrary", "arbitrary")),
        name="token_mixers",
    )(*([proj2d] * len(attn_specs)), proj3d, proj3d, proj3d, intra, qd, kd, cd)
    return attn, ret.reshape(t, RET_HEADS * RET_VAL_DIM)


def _retention_decays():
    c = RET_CHUNK
    log_gamma = jnp.log1p(-jnp.exp2(-5.0 - jnp.arange(RET_HEADS, dtype=F32)))
    idx = jnp.arange(c, dtype=F32)
    rel = idx[:, None] - idx[None, :]
    intra = jnp.where(rel >= 0, jnp.exp(log_gamma[:, None, None] * jnp.maximum(rel, 0.0)), 0.0)
    qd = jnp.exp(log_gamma[:, None] * (idx + 1.0))[:, :, None]
    kd = jnp.exp(log_gamma[:, None] * (c - 1.0 - idx))[:, :, None]
    cd = jnp.exp(log_gamma * c)[:, None, None]
    qd = jnp.broadcast_to(qd, (RET_HEADS, c, RET_KEY_DIM))
    kd = jnp.broadcast_to(kd, (RET_HEADS, c, RET_KEY_DIM))
    cd = jnp.broadcast_to(cd, (RET_HEADS, 1, RET_VAL_DIM))
    return intra, qd, kd, cd


def _merge_kernel(attn_ref, ret_ref, rg0_ref, rg1_ref, ga_ref, gr_ref, x_ref, p_ref, wa_ref, wr_ref, wo_ref,
                  wpg_ref, wpp_ref, gng_ref, gnb_ref, g1_ref, b1_ref, x1b_ref, resid_ref, *, sub):
    blocks = [slice(r, r + sub) for r in range(0, attn_ref.shape[0], sub)]

    def retention_proj(rs):
        pr = None
        for h in range(RET_HEADS):
            cs = slice(h * RET_VAL_DIM, (h + 1) * RET_VAL_DIM)
            y = ret_ref[rs, cs].astype(F32)
            mu = jnp.mean(y, axis=-1, keepdims=True)
            d = y - mu
            var = jnp.mean(d * d, axis=-1, keepdims=True)
            yn = d * lax.rsqrt(var + GN_EPS) * gng_ref[:, cs] + gnb_ref[:, cs]
            half = RET_HEADS // 2
            g_ref = rg0_ref if h < half else rg1_ref
            g = g_ref[rs, (h % half) * RET_VAL_DIM:(h % half + 1) * RET_VAL_DIM].astype(F32)
            part = _dot(((g * _sigmoid(g)) * yn).astype(BF16), wr_ref[cs, :])
            pr = part if pr is None else pr + part
        return pr

    first, ret = [], []
    for rs in blocks:
        pa = _dot(attn_ref[rs, :], wa_ref[...])
        ret.append(retention_proj(rs))
        first.append((pa, _dot(p_ref[rs, :].astype(BF16), wpp_ref[...])))
    mix = []
    for rs, (pa, _), pr in zip(blocks, first, ret):
        merged = _sigmoid(ga_ref[rs, :].astype(F32)) * pa + _sigmoid(gr_ref[rs, :].astype(F32)) * pr
        mix.append(_dot(merged.astype(BF16), wo_ref[...]))
    x1s, gates = [], []
    for rs, m in zip(blocks, mix):
        x1 = _layer_norm(DN_ALPHA * x_ref[rs, :] + m, g1_ref[...], b1_ref[...])
        x1b = x1.astype(BF16)
        x1b_ref[rs, :] = x1b
        x1s.append(x1)
        gates.append(_dot(x1b, wpg_ref[...]))
    for rs, x1, gate, (_, pp) in zip(blocks, x1s, gates, first):
        resid_ref[rs, :] = DN_ALPHA * x1 + _sigmoid(gate) * pp


def _merge(attn, ret, proj2d, x2d, p3d, wa, wr, wo, wpg, wpp, gn_g, gn_b, g1, b1, layer, tm=512, sub=256):
    t = x2d.shape[0]

    def rows(width, cb=0):
        return pl.BlockSpec((tm, width), lambda i: (i, cb))

    ret_width = RET_HEADS * RET_VAL_DIM
    gate_width = ret_width // 2
    assert COL_RET_G % gate_width == 0
    return pl.pallas_call(
        functools.partial(_merge_kernel, sub=sub),
        grid=(t // tm,),
        in_specs=[rows(ATTN_WIDTH), rows(ret_width),
                  rows(gate_width, COL_RET_G // gate_width), rows(gate_width, COL_RET_G // gate_width + 1),
                  rows(D_MODEL, COL_GATE_A // D_MODEL), rows(D_MODEL, COL_GATE_R // D_MODEL),
                  rows(D_MODEL), pl.BlockSpec((None, tm, PLE_DIM), lambda i: (layer, i, 0)),
                  _resident(wa, layer), _resident(wr, layer), _resident(wo, layer), _resident(wpg, layer),
                  _resident(wpp, layer), _resident(gn_g, layer), _resident(gn_b, layer),
                  _resident(g1, layer), _resident(b1, layer)],
        out_specs=[rows(D_MODEL), rows(D_MODEL)],
        out_shape=[jax.ShapeDtypeStruct((t, D_MODEL), BF16), jax.ShapeDtypeStruct((t, D_MODEL), F32)],
        compiler_params=_params(("arbitrary",)),
        name="merge_ln1_ple",
    )(attn, ret, proj2d, proj2d, proj2d, proj2d, x2d, p3d, wa, wr, wo, wpg, wpp, gn_g, gn_b, g1, b1)


FFN_HALO = 16


def _ffn_kernel(x_ref, halo_ref, wup_ref, cw_ref, cb_ref, wd_ref, resid_ref, g2_ref, b2_ref, o_ref,
                xcat_ref, *, tm, sub, tf, tiles_per_seq):
    first = (pl.program_id(0) % tiles_per_seq) == 0
    halo = halo_ref[...]
    xcat_ref[0:FFN_HALO, :] = jnp.where(first, jnp.zeros_like(halo), halo)
    xcat_ref[FFN_HALO:, :] = x_ref[...]

    c0 = 0.7978845608028654
    pairs = [(r, c) for r in range(tm // sub) for c in range(D_FF // tf)]

    def up_proj(k):
        r, c = pairs[k]
        xc = xcat_ref[r * sub:r * sub + sub + FFN_HALO, :]
        return (_dot(xc, wup_ref[:, c * tf:(c + 1) * tf]),
                _dot(xc, wup_ref[:, D_FF + c * tf:D_FF + (c + 1) * tf]))

    def conv(h, col0):
        cs = slice(col0, col0 + tf)
        return (cw_ref[2:3, cs] * h[FFN_HALO:FFN_HALO + sub, :]
                + cw_ref[1:2, cs] * h[FFN_HALO - 1:FFN_HALO - 1 + sub, :]
                + cw_ref[0:1, cs] * h[FFN_HALO - 2:FFN_HALO - 2 + sub, :]
                + cb_ref[:, cs])

    ahead = 2
    h_queue = [up_proj(k) for k in range(ahead)]
    acc = None
    for k, (r, c) in enumerate(pairs):
        hg, hu = h_queue.pop(0)
        if k + ahead < len(pairs):
            h_queue.append(up_proj(k + ahead))
        gate = conv(hg, c * tf)
        up = conv(hu, D_FF + c * tf)
        half_gate = 0.5 * gate
        gelu = half_gate + half_gate * jnp.tanh(gate * (c0 + (c0 * 0.044715) * (gate * gate)))
        d = _dot((gelu * up).astype(BF16), wd_ref[c * tf:(c + 1) * tf, :])
        acc = d if c == 0 else acc + d
        if c == D_FF // tf - 1:
            rs = slice(r * sub, (r + 1) * sub)
            o_ref[rs, :] = _layer_norm(resid_ref[rs, :] + acc, g2_ref[...], b2_ref[...])


def _ffn(x1b, resid, w_up, conv_w, conv_b, w_down, g2, b2, layer, seq, tm=512, tf=256, sub=256):
    t = x1b.shape[0]
    hb = tm // FFN_HALO

    return pl.pallas_call(
        functools.partial(_ffn_kernel, tm=tm, sub=sub, tf=tf, tiles_per_seq=seq // tm),
        grid=(t // tm,),
        in_specs=[pl.BlockSpec((tm, D_MODEL), lambda i: (i, 0)),
                  pl.BlockSpec((FFN_HALO, D_MODEL), lambda i: (jnp.maximum(i * hb - 1, 0), 0)),
                  _resident(w_up, layer), _resident(conv_w, layer), _resident(conv_b, layer),
                  _resident(w_down, layer),
                  pl.BlockSpec((tm, D_MODEL), lambda i: (i, 0)),
                  _resident(g2, layer), _resident(b2, layer)],
        out_specs=pl.BlockSpec((tm, D_MODEL), lambda i: (i, 0)),
        out_shape=jax.ShapeDtypeStruct((t, D_MODEL), F32),
        scratch_shapes=[pltpu.VMEM((tm + FFN_HALO, D_MODEL), BF16)],
        compiler_params=_params(("arbitrary",)),
        name="conv_ffn_ln2",
    )(x1b, x1b, w_up, conv_w, conv_b, w_down, resid, g2, b2)


def kernel(x, p, positions, w_in, w_attn_proj, w_ret_proj, ret_gn_g, ret_gn_b, w_out, ln1_g, ln1_b, w_up,
           conv_w, conv_b, w_down, w_ple_gate, w_ple_proj, ln2_g, ln2_b):
    b, s, d = x.shape
    t = b * s
    assert s % ATTN_TILE == 0
    cos, sin = _rope_tables(positions)
    decays = _retention_decays()
    col_scale = _proj_col_scale()
    perms = _row_permutations()
    x2d = x.reshape(t, d)
    depth = w_in.shape[0]
    p3d = p.reshape(depth, t, PLE_DIM)
    row = lambda v: v.reshape(depth, 1, -1)
    wa, wr, wo = w_attn_proj.astype(BF16), w_ret_proj.astype(BF16), w_out.astype(BF16)
    wpg, wpp = w_ple_gate.astype(BF16), w_ple_proj.astype(BF16)
    wup, wdn = w_up.astype(BF16), w_down.astype(BF16)
    for i in range(depth):
        proj2d = _in_proj(x2d, w_in, i, col_scale, cos, sin, perms)
        attn, ret = _token_mixers(proj2d, decays, b, s)
        x1b, resid = _merge(attn, ret, proj2d, x2d, p3d, wa, wr, wo, wpg, wpp, row(ret_gn_g), row(ret_gn_b),
                            row(ln1_g), row(ln1_b), i)
        x2d = _ffn(x1b, resid, wup, conv_w, row(conv_b), wdn, row(ln2_g), row(ln2_b), i, s)
    return x2d.reshape(b, s, d)
```

```python
import functools

import numpy as np
import jax
import jax.numpy as jnp
from jax import lax
from jax.experimental import pallas as pl
from jax.experimental.pallas import tpu as pltpu

F32 = jnp.float32
BF16 = jnp.bfloat16

D_MODEL = 1024
ATTN_DILATIONS = (1, 4, 16)
ATTN_HEADS = 8
HEAD_DIM = 128
ATTN_WIDTH = ATTN_HEADS * HEAD_DIM
RET_HEADS = 4
RET_KEY_DIM = 256
RET_VAL_DIM = 512
RET_CHUNK = 128
ROPE_BASE = 10000.0
D_FF = 2816
PLE_DIM = 256
N_IN = 17408
DEPTH = 2
DN_ALPHA = (2 * DEPTH) ** 0.25
LN_EPS = 1e-5
GN_EPS = 1e-6
LOG2E = 1.4426950408889634

COL_RET_Q = 3 * 3 * ATTN_WIDTH
COL_RET_K = COL_RET_Q + RET_HEADS * RET_KEY_DIM
COL_RET_V = COL_RET_K + RET_HEADS * RET_KEY_DIM
COL_RET_G = COL_RET_V + RET_HEADS * RET_VAL_DIM
COL_GATE_A = COL_RET_G + RET_HEADS * RET_VAL_DIM
COL_GATE_R = COL_GATE_A + D_MODEL

LANES = 128
ATTN_BLK = 128
ATTN_TILE = ATTN_BLK * max(ATTN_DILATIONS)
VMEM_LIMIT = 56 * 1024 * 1024


def _params(sem, vmem=VMEM_LIMIT, flags=None):
    return pltpu.CompilerParams(dimension_semantics=sem, vmem_limit_bytes=vmem, flags=flags)


def _resident(arr, layer=None):
    if layer is None:
        return pl.BlockSpec(arr.shape, lambda *_: (0,) * arr.ndim, pipeline_mode=pl.Buffered(1))
    return pl.BlockSpec((None,) + arr.shape[1:], lambda *_: (layer,) + (0,) * (arr.ndim - 1),
                        pipeline_mode=pl.Buffered(1))


def _dot(a, b):
    return jnp.dot(a, b, preferred_element_type=F32)


def _dot_nt(a, b):
    return lax.dot_general(a, b, (((1,), (1,)), ((), ())), preferred_element_type=F32)


def _dot_tn(a, b):
    return lax.dot_general(a, b, (((0,), (0,)), ((), ())), preferred_element_type=F32)


def _sigmoid(x):
    return 1.0 / (1.0 + jnp.exp2(x * -LOG2E))


def _layer_norm(y, g, b):
    mu = jnp.mean(y, axis=-1, keepdims=True)
    d = y - mu
    var = jnp.mean(d * d, axis=-1, keepdims=True)
    return d * lax.rsqrt(var + LN_EPS) * g + b


def _rope_kernel(pos_ref, freq_ref, cos_ref, sin_ref):
    ang = pos_ref[...] * freq_ref[...]
    cos_ref[...] = jnp.cos(ang)
    sin_ref[...] = jnp.sin(ang)


def _rope_tables(positions):
    b, s = positions.shape
    half = RET_KEY_DIM // 2
    pos = jnp.broadcast_to(positions.astype(F32).reshape(b * s, 1), (b * s, half))
    freq = jnp.power(ROPE_BASE, -jnp.arange(half, dtype=F32) / half).reshape(1, half)
    rows = 2048
    spec = pl.BlockSpec((rows, half), lambda i: (i, 0))
    cos, sin = pl.pallas_call(
        _rope_kernel,
        grid=(b * s // rows,),
        in_specs=[spec, pl.BlockSpec((1, half), lambda i: (0, 0))],
        out_specs=[spec, spec],
        out_shape=[jax.ShapeDtypeStruct((b * s, half), F32)] * 2,
        compiler_params=_params(("arbitrary",)),
        name="rope_tables",
    )(pos, freq)
    return cos, sin


PERM_BLK = 256


def _proj_kernel(x_ref, w_ref, cs_ref, cos_ref, sin_ref, perm_ref, o_ref, xb_ref, *, tm, tn):
    j = pl.program_id(1)

    @pl.when(j == 0)
    def _():
        xb_ref[0] = x_ref[...].astype(BF16)
        for blk in range(tm // PERM_BLK):
            xblk = xb_ref[0, blk * PERM_BLK:(blk + 1) * PERM_BLK, :]
            for gi, dil in enumerate(ATTN_DILATIONS):
                if dil == 1:
                    continue
                y = _dot(perm_ref[gi - 1], xblk).astype(BF16)
                n = PERM_BLK // dil
                for r in range(dil):
                    dst = r * (tm // dil) + blk * n
                    xb_ref[gi, dst:dst + n, :] = y[r * n:(r + 1) * n, :]

    grp = (j * tn) // (3 * ATTN_WIDTH)
    sel = jnp.where(grp < len(ATTN_DILATIONS), grp, 0)
    is_rope = jnp.logical_and(j >= COL_RET_Q // tn, j < COL_RET_V // tn)

    @pl.when(jnp.logical_not(is_rope))
    def _():
        o_ref[...] = (_dot(xb_ref[sel], w_ref[...].astype(BF16)) * cs_ref[...]).astype(o_ref.dtype)

    @pl.when(is_rope)
    def _():
        y = _dot(xb_ref[0], w_ref[...].astype(BF16)) * cs_ref[...]
        cos, sin = cos_ref[...], sin_ref[...]
        half = RET_KEY_DIM // 2
        for h0 in range(0, tn, RET_KEY_DIM):
            t1, t2 = y[:, h0:h0 + half], y[:, h0 + half:h0 + RET_KEY_DIM]
            o_ref[:, h0:h0 + half] = (t1 * cos - t2 * sin).astype(o_ref.dtype)
            o_ref[:, h0 + half:h0 + RET_KEY_DIM] = (t1 * sin + t2 * cos).astype(o_ref.dtype)


def _proj_col_scale():
    scale = np.ones((1, N_IN), np.float32)
    for g in range(len(ATTN_DILATIONS)):
        scale[:, g * 3 * ATTN_WIDTH:g * 3 * ATTN_WIDTH + ATTN_WIDTH] = (HEAD_DIM ** -0.5) * LOG2E
    scale[:, COL_RET_K:COL_RET_V] = RET_KEY_DIM ** -0.5
    return jnp.asarray(scale)


def _row_permutations():
    mats = []
    for dil in ATTN_DILATIONS[1:]:
        m = np.zeros((PERM_BLK, PERM_BLK), np.float32)
        u = np.arange(PERM_BLK)
        m[(u % dil) * (PERM_BLK // dil) + u // dil, u] = 1.0
        mats.append(m)
    return jnp.asarray(np.stack(mats), dtype=BF16)


def _in_proj(x2d, w_in, layer, col_scale, cos, sin, perms, tm=ATTN_TILE, tn=1024):
    t, d = x2d.shape
    n = w_in.shape[2]
    half = RET_KEY_DIM // 2
    assert COL_RET_Q % tn == 0 and COL_RET_V % tn == 0 and tn % RET_KEY_DIM == 0

    def per_row_tile(width):
        return pl.BlockSpec((tm, width), lambda i, j: (i, 0))

    return pl.pallas_call(
        functools.partial(_proj_kernel, tm=tm, tn=tn),
        grid=(t // tm, n // tn),
        in_specs=[pl.BlockSpec((tm, d), lambda i, j: (i, 0)),
                  pl.BlockSpec((None, d, tn), lambda i, j: (layer, 0, j)),
                  pl.BlockSpec((1, tn), lambda i, j: (0, j)),
                  per_row_tile(half), per_row_tile(half), _resident(perms)],
        out_specs=pl.BlockSpec((tm, tn), lambda i, j: (i, j)),
        out_shape=jax.ShapeDtypeStruct((t, n), BF16),
        scratch_shapes=[pltpu.VMEM((len(ATTN_DILATIONS), tm, d), BF16)],
        compiler_params=_params(("arbitrary", "arbitrary")),
        name="in_proj",
    )(x2d, w_in, col_scale, cos, sin, perms)


def _attn_blocks(refs, o_ref, o_scr, lse_scr, out_scr, has_prev):
    (q0, k0, v0, q1, k1, v1, q2, k2, v2, kp0, vp0) = refs[:11]
    n1 = ATTN_DILATIONS[1]
    kp1 = refs[11:11 + n1]
    vp1 = refs[11 + n1:11 + 2 * n1]
    kp2, vp2 = refs[11 + 2 * n1:]
    blk = ATTN_BLK

    def rows(start):
        return slice(start, start + blk)

    row = lax.broadcasted_iota(jnp.int32, (blk, 2 * blk), 0)
    col = lax.broadcasted_iota(jnp.int32, (blk, 2 * blk), 1)
    dist = row + blk - col
    band = jnp.logical_and(dist >= 0, dist <= blk)
    bias_mid = jnp.where(band, 0.0, -jnp.inf)
    bias_first = jnp.where(jnp.logical_and(band, jnp.logical_or(col >= blk, has_prev)), 0.0, -jnp.inf)
    ones = jnp.ones((2 * blk, HEAD_DIM), BF16)

    blocks = []
    for n in range(ATTN_TILE // blk):
        blocks.append((q0, k0, v0, n * blk, (kp0, vp0, 0) if n == 0 else None, 0))
    for gi, (q, k, v) in ((1, (q1, k1, v1)), (2, (q2, k2, v2))):
        dil = ATTN_DILATIONS[gi]
        per = ATTN_TILE // dil
        for r in range(dil):
            for n in range(per // blk):
                base = r * per + n * blk
                prev = None
                if n == 0:
                    prev = (kp1[r], vp1[r], 0) if gi == 1 else (kp2, vp2, base)
                blocks.append((q, k, v, base, prev, gi))

    def prev_and_cur(b, which):
        base, prev = b[3], b[4]
        if prev is None:
            return b[which][base - blk:base + blk, :]
        return jnp.concatenate([prev[which - 1][rows(prev[2]), :], b[which][rows(base), :]], axis=0)

    def scores(b):
        return _dot_nt(b[0][rows(b[3]), :], prev_and_cur(b, 1))

    def softmax_pv(s, b):
        m = jnp.max(s, axis=-1, keepdims=True)
        p = jnp.exp2(s - m)
        pv = _dot(p.astype(BF16), jnp.concatenate([prev_and_cur(b, 2), ones], axis=1))
        l = pv[:, HEAD_DIM:]
        return pv[:, :HEAD_DIM] * (1.0 / l), m + jnp.log2(l)

    n_stored = len(blocks) - ATTN_DILATIONS[2]
    ahead = 1
    s_queue = [scores(blocks[i]) for i in range(ahead)]
    for i, b in enumerate(blocks):
        s = s_queue.pop(0) + (bias_mid if b[4] is None else bias_first)
        if i + ahead < len(blocks):
            s_queue.append(scores(blocks[i + ahead]))
        o2, lse2 = softmax_pv(s, b)
        if i < n_stored:
            gi = b[5]
            o_scr[gi, rows(b[3]), :] = o2
            lse_scr[gi, rows(b[3]), :] = lse2
            yield
            continue
        r = i - n_stored
        d1, d2 = ATTN_DILATIONS[1], ATTN_DILATIONS[2]
        src0 = pl.ds(r, blk, stride=d2)
        src1 = pl.ds((r % d1) * (ATTN_TILE // d1) + r // d1, blk, stride=d2 // d1)
        lse0, lse1 = lse_scr[0, src0, :], lse_scr[1, src1, :]
        mx = jnp.maximum(jnp.maximum(lse0, lse1), lse2)
        e0, e1, e2 = jnp.exp2(lse0 - mx), jnp.exp2(lse1 - mx), jnp.exp2(lse2 - mx)
        num = e0 * o_scr[0, src0, :] + e1 * o_scr[1, src1, :] + e2 * o2
        out_scr[src0, :] = num * (1.0 / (e0 + e1 + e2))
        yield
    o_ref[...] = out_scr[...].astype(o_ref.dtype)


def _ret_chunks(q_ref, k_ref, v_ref, intra_ref, qd_ref, kd_ref, cd_ref, o_ref, state_ref, rows):
    n_chunks = rows // RET_CHUNK

    def chunk(c):
        return slice(c * RET_CHUNK, (c + 1) * RET_CHUNK)

    def independent(c):
        q, k = q_ref[0, chunk(c), :], k_ref[0, chunk(c), :]
        kd = (k.astype(F32) * kd_ref[0]).astype(BF16)
        return _dot_nt(q, k), _dot_tn(kd, v_ref[0, chunk(c), :])

    ahead = 2
    queue = [independent(c) for c in range(ahead)]
    state = state_ref[...]
    for c in range(n_chunks):
        scores, update = queue.pop(0)
        qd = (q_ref[0, chunk(c), :].astype(F32) * qd_ref[0]).astype(BF16)
        y = _dot((scores * intra_ref[0]).astype(BF16), v_ref[0, chunk(c), :]) + _dot(qd, state.astype(BF16))
        if c + ahead < n_chunks:
            queue.append(independent(c + ahead))
        state = cd_ref[0] * state + update
        o_ref[0, chunk(c), :] = y.astype(o_ref.dtype)
        if c + 1 == n_chunks:
            state_ref[...] = state
        yield


def _mixer_kernel(*refs, n_attn_in, rows):
    attn_in = refs[:n_attn_in]
    ret_in = refs[n_attn_in:n_attn_in + 7]
    attn_out, ret_out, o_scr, lse_scr, out_scr, state_ref = refs[n_attn_in + 7:]
    n = pl.program_id(2)

    @pl.when(n == 0)
    def _():
        state_ref[...] = jnp.zeros_like(state_ref)

    steps_per_tile = ATTN_TILE // rows
    attn = _attn_blocks(attn_in, attn_out, o_scr, lse_scr, out_scr, n // steps_per_tile > 0)
    ret = _ret_chunks(*ret_in, ret_out, state_ref, rows)
    for _ in ret:
        pass
    for _ in attn:
        pass


def _token_mixers(proj2d, decays, batch, seq, rows=1024):
    t = proj2d.shape[0]
    intra, qd, kd, cd = decays
    tiles_per_seq = seq // ATTN_TILE
    steps_per_tile = ATTN_TILE // rows
    heads_per_step = ATTN_HEADS // RET_HEADS
    assert steps_per_tile == heads_per_step and seq % ATTN_TILE == 0
    heads_per_group = 3 * ATTN_HEADS
    n1, n2 = ATTN_DILATIONS[1], ATTN_DILATIONS[2]
    blocks_per_tile = ATTN_TILE // ATTN_BLK

    def tile(bi, n):
        return bi * tiles_per_seq + n // steps_per_tile

    def prev_tile(bi, n):
        return bi * tiles_per_seq + jnp.maximum(n // steps_per_tile - 1, 0)

    def head(hr, n):
        return (n % steps_per_tile) * RET_HEADS + hr

    def col(gi, which, hr, n):
        return gi * heads_per_group + which * ATTN_HEADS + head(hr, n)

    def cur(gi, which):
        return pl.BlockSpec((ATTN_TILE, HEAD_DIM), lambda bi, hr, n: (tile(bi, n), col(gi, which, hr, n)))

    def prev_blk(gi, which, blk_in_tile):
        return pl.BlockSpec((ATTN_BLK, HEAD_DIM),
                            lambda bi, hr, n: (prev_tile(bi, n) * blocks_per_tile + blk_in_tile,
                                               col(gi, which, hr, n)))

    def prev_whole(gi, which):
        return pl.BlockSpec((ATTN_TILE, HEAD_DIM), lambda bi, hr, n: (prev_tile(bi, n), col(gi, which, hr, n)))

    per1 = ATTN_TILE // n1 // ATTN_BLK
    attn_specs = [cur(g, w) for g in range(3) for w in range(3)]
    attn_specs += [prev_blk(0, 1, blocks_per_tile - 1), prev_blk(0, 2, blocks_per_tile - 1)]
    attn_specs += [prev_blk(1, 1, r * per1 + per1 - 1) for r in range(n1)]
    attn_specs += [prev_blk(1, 2, r * per1 + per1 - 1) for r in range(n1)]
    attn_specs += [prev_whole(2, 1), prev_whole(2, 2)]
    assert ATTN_TILE // n2 == ATTN_BLK

    steps_per_seq = seq // rows
    qb, kb, vb = COL_RET_Q // RET_KEY_DIM, COL_RET_K // RET_KEY_DIM, COL_RET_V // RET_VAL_DIM
    c = RET_CHUNK

    def ret_rows(width, cb):
        return pl.BlockSpec((1, rows, width), lambda bi, hr, n: (bi * steps_per_seq + n, 0, cb + hr))

    def per_head(shape):
        return pl.BlockSpec((1,) + shape, lambda bi, hr, n: (hr, 0, 0))

    proj3d = proj2d.reshape(t // rows, rows, N_IN)
    ret_specs = [ret_rows(RET_KEY_DIM, qb), ret_rows(RET_KEY_DIM, kb), ret_rows(RET_VAL_DIM, vb),
                 per_head((c, c)), per_head((c, RET_KEY_DIM)), per_head((c, RET_KEY_DIM)),
                 per_head((1, RET_VAL_DIM))]
    attn, ret = pl.pallas_call(
        functools.partial(_mixer_kernel, n_attn_in=len(attn_specs), rows=rows),
        grid=(batch, RET_HEADS, steps_per_seq),
        in_specs=attn_specs + ret_specs,
        out_specs=[pl.BlockSpec((ATTN_TILE, HEAD_DIM), lambda bi, hr, n: (tile(bi, n), head(hr, n))),
                   ret_rows(RET_VAL_DIM, 0)],
        out_shape=[jax.ShapeDtypeStruct((t, ATTN_WIDTH), BF16),
                   jax.ShapeDtypeStruct((t // rows, rows, RET_HEADS * RET_VAL_DIM), BF16)],
        scratch_shapes=[pltpu.VMEM((2, ATTN_TILE, HEAD_DIM), F32)] * 2
                       + [pltpu.VMEM((ATTN_TILE, HEAD_DIM), F32), pltpu.VMEM((RET_KEY_DIM, RET_VAL_DIM), F32)],
        compiler_params=_params(("arbitrary", "arbitrary", "arbitrary")),
        name="token_mixers",
    )(*([proj2d] * len(attn_specs)), proj3d, proj3d, proj3d, intra, qd, kd, cd)
    return attn, ret.reshape(t, RET_HEADS * RET_VAL_DIM)


def _retention_decays():
    c = RET_CHUNK
    log_gamma = jnp.log1p(-jnp.exp2(-5.0 - jnp.arange(RET_HEADS, dtype=F32)))
    idx = jnp.arange(c, dtype=F32)
    rel = idx[:, None] - idx[None, :]
    intra = jnp.where(rel >= 0, jnp.exp(log_gamma[:, None, None] * jnp.maximum(rel, 0.0)), 0.0)
    qd = jnp.exp(log_gamma[:, None] * (idx + 1.0))[:, :, None]
    kd = jnp.exp(log_gamma[:, None] * (c - 1.0 - idx))[:, :, None]
    cd = jnp.exp(log_gamma * c)[:, None, None]
    qd = jnp.broadcast_to(qd, (RET_HEADS, c, RET_KEY_DIM))
    kd = jnp.broadcast_to(kd, (RET_HEADS, c, RET_KEY_DIM))
    cd = jnp.broadcast_to(cd, (RET_HEADS, 1, RET_VAL_DIM))
    return intra, qd, kd, cd


def _merge_kernel(attn_ref, ret_ref, rg0_ref, rg1_ref, ga_ref, gr_ref, x_ref, p_ref, wa_ref, wr_ref, wo_ref,
                  wpg_ref, wpp_ref, gng_ref, gnb_ref, g1_ref, b1_ref, x1b_ref, resid_ref, *, sub):
    blocks = [slice(r, r + sub) for r in range(0, attn_ref.shape[0], sub)]

    def retention_proj(rs):
        pr = None
        for h in range(RET_HEADS):
            cs = slice(h * RET_VAL_DIM, (h + 1) * RET_VAL_DIM)
            y = ret_ref[rs, cs].astype(F32)
            mu = jnp.mean(y, axis=-1, keepdims=True)
            d = y - mu
            var = jnp.mean(d * d, axis=-1, keepdims=True)
            yn = d * lax.rsqrt(var + GN_EPS) * gng_ref[:, cs] + gnb_ref[:, cs]
            half = RET_HEADS // 2
            g_ref = rg0_ref if h < half else rg1_ref
            g = g_ref[rs, (h % half) * RET_VAL_DIM:(h % half + 1) * RET_VAL_DIM].astype(F32)
            part = _dot(((g * _sigmoid(g)) * yn).astype(BF16), wr_ref[cs, :])
            pr = part if pr is None else pr + part
        return pr

    first, ret = [], []
    for rs in blocks:
        pa = _dot(attn_ref[rs, :], wa_ref[...])
        ret.append(retention_proj(rs))
        first.append((pa, _dot(p_ref[rs, :].astype(BF16), wpp_ref[...])))
    mix = []
    for rs, (pa, _), pr in zip(blocks, first, ret):
        merged = _sigmoid(ga_ref[rs, :].astype(F32)) * pa + _sigmoid(gr_ref[rs, :].astype(F32)) * pr
        mix.append(_dot(merged.astype(BF16), wo_ref[...]))
    x1s, gates = [], []
    for rs, m in zip(blocks, mix):
        x1 = _layer_norm(DN_ALPHA * x_ref[rs, :] + m, g1_ref[...], b1_ref[...])
        x1b = x1.astype(BF16)
        x1b_ref[rs, :] = x1b
        x1s.append(x1)
        gates.append(_dot(x1b, wpg_ref[...]))
    for rs, x1, gate, (_, pp) in zip(blocks, x1s, gates, first):
        resid_ref[rs, :] = DN_ALPHA * x1 + _sigmoid(gate) * pp


def _merge(attn, ret, proj2d, x2d, p3d, wa, wr, wo, wpg, wpp, gn_g, gn_b, g1, b1, layer, tm=512, sub=256):
    t = x2d.shape[0]

    def rows(width, cb=0):
        return pl.BlockSpec((tm, width), lambda i: (i, cb))

    ret_width = RET_HEADS * RET_VAL_DIM
    gate_width = ret_width // 2
    assert COL_RET_G % gate_width == 0
    return pl.pallas_call(
        functools.partial(_merge_kernel, sub=sub),
        grid=(t // tm,),
        in_specs=[rows(ATTN_WIDTH), rows(ret_width),
                  rows(gate_width, COL_RET_G // gate_width), rows(gate_width, COL_RET_G // gate_width + 1),
                  rows(D_MODEL, COL_GATE_A // D_MODEL), rows(D_MODEL, COL_GATE_R // D_MODEL),
                  rows(D_MODEL), pl.BlockSpec((None, tm, PLE_DIM), lambda i: (layer, i, 0)),
                  _resident(wa, layer), _resident(wr, layer), _resident(wo, layer), _resident(wpg, layer),
                  _resident(wpp, layer), _resident(gn_g, layer), _resident(gn_b, layer),
                  _resident(g1, layer), _resident(b1, layer)],
        out_specs=[rows(D_MODEL), rows(D_MODEL)],
        out_shape=[jax.ShapeDtypeStruct((t, D_MODEL), BF16), jax.ShapeDtypeStruct((t, D_MODEL), F32)],
        compiler_params=_params(("arbitrary",)),
        name="merge_ln1_ple",
    )(attn, ret, proj2d, proj2d, proj2d, proj2d, x2d, p3d, wa, wr, wo, wpg, wpp, gn_g, gn_b, g1, b1)


FFN_HALO = 16


def _ffn_kernel(x_ref, halo_ref, wup_ref, cw_ref, cb_ref, wd_ref, resid_ref, g2_ref, b2_ref, o_ref,
                xcat_ref, *, tm, sub, tf, tiles_per_seq):
    first = (pl.program_id(0) % tiles_per_seq) == 0
    halo = halo_ref[...]
    xcat_ref[0:FFN_HALO, :] = jnp.where(first, jnp.zeros_like(halo), halo)
    xcat_ref[FFN_HALO:, :] = x_ref[...]

    c0 = 0.7978845608028654
    pairs = [(r, c) for r in range(tm // sub) for c in range(D_FF // tf)]

    def up_proj(k):
        r, c = pairs[k]
        xc = xcat_ref[r * sub:r * sub + sub + FFN_HALO, :]
        return (_dot(xc, wup_ref[:, c * tf:(c + 1) * tf]),
                _dot(xc, wup_ref[:, D_FF + c * tf:D_FF + (c + 1) * tf]))

    def conv(h, col0):
        cs = slice(col0, col0 + tf)
        return (cw_ref[2:3, cs] * h[FFN_HALO:FFN_HALO + sub, :]
                + cw_ref[1:2, cs] * h[FFN_HALO - 1:FFN_HALO - 1 + sub, :]
                + cw_ref[0:1, cs] * h[FFN_HALO - 2:FFN_HALO - 2 + sub, :]
                + cb_ref[:, cs])

    ahead = 2
    h_queue = [up_proj(k) for k in range(ahead)]
    acc = None
    for k, (r, c) in enumerate(pairs):
        hg, hu = h_queue.pop(0)
        if k + ahead < len(pairs):
            h_queue.append(up_proj(k + ahead))
        gate = conv(hg, c * tf)
        up = conv(hu, D_FF + c * tf)
        half_gate = 0.5 * gate
        gelu = half_gate + half_gate * jnp.tanh(gate * (c0 + (c0 * 0.044715) * (gate * gate)))
        d = _dot((gelu * up).astype(BF16), wd_ref[c * tf:(c + 1) * tf, :])
        acc = d if c == 0 else acc + d
        if c == D_FF // tf - 1:
            rs = slice(r * sub, (r + 1) * sub)
            o_ref[rs, :] = _layer_norm(resid_ref[rs, :] + acc, g2_ref[...], b2_ref[...])


def _ffn(x1b, resid, w_up, conv_w, conv_b, w_down, g2, b2, layer, seq, tm=512, tf=256, sub=256):
    t = x1b.shape[0]
    hb = tm // FFN_HALO

    return pl.pallas_call(
        functools.partial(_ffn_kernel, tm=tm, sub=sub, tf=tf, tiles_per_seq=seq // tm),
        grid=(t // tm,),
        in_specs=[pl.BlockSpec((tm, D_MODEL), lambda i: (i, 0)),
                  pl.BlockSpec((FFN_HALO, D_MODEL), lambda i: (jnp.maximum(i * hb - 1, 0), 0)),
                  _resident(w_up, layer), _resident(conv_w, layer), _resident(conv_b, layer),
                  _resident(w_down, layer),
                  pl.BlockSpec((tm, D_MODEL), lambda i: (i, 0)),
                  _resident(g2, layer), _resident(b2, layer)],
        out_specs=pl.BlockSpec((tm, D_MODEL), lambda i: (i, 0)),
        out_shape=jax.ShapeDtypeStruct((t, D_MODEL), F32),
        scratch_shapes=[pltpu.VMEM((tm + FFN_HALO, D_MODEL), BF16)],
        compiler_params=_params(("arbitrary",)),
        name="conv_ffn_ln2",
    )(x1b, x1b, w_up, conv_w, conv_b, w_down, resid, g2, b2)


def kernel(x, p, positions, w_in, w_attn_proj, w_ret_proj, ret_gn_g, ret_gn_b, w_out, ln1_g, ln1_b, w_up,
           conv_w, conv_b, w_down, w_ple_gate, w_ple_proj, ln2_g, ln2_b):
    b, s, d = x.shape
    t = b * s
    assert s % ATTN_TILE == 0
    cos, sin = _rope_tables(positions)
    decays = _retention_decays()
    col_scale = _proj_col_scale()
    perms = _row_permutations()
    x2d = x.reshape(t, d)
    depth = w_in.shape[0]
    p3d = p.reshape(depth, t, PLE_DIM)
    row = lambda v: v.reshape(depth, 1, -1)
    wa, wr, wo = w_attn_proj.astype(BF16), w_ret_proj.astype(BF16), w_out.astype(BF16)
    wpg, wpp = w_ple_gate.astype(BF16), w_ple_proj.astype(BF16)
    wup, wdn, win = w_up.astype(BF16), w_down.astype(BF16), w_in.astype(BF16)
    for i in range(depth):
        proj2d = _in_proj(x2d, win, i, col_scale, cos, sin, perms)
        attn, ret = _token_mixers(proj2d, decays, b, s)
        x1b, resid = _merge(attn, ret, proj2d, x2d, p3d, wa, wr, wo, wpg, wpp, row(ret_gn_g), row(ret_gn_b),
                            row(ln1_g), row(ln1_b), i)
        x2d = _ffn(x1b, resid, wup, conv_w, row(conv_b), wdn, row(ln2_g), row(ln2_b), i, s)
    return x2d.reshape(b, s, d)
```
